```python
import math
import jax, jax.numpy as jnp
from jax import lax
import numpy as np

D_MODEL = 1024
BATCH = 1
SEQ = 16384
DEPTH = 4
DEC_BATCH = 8
DEC_SEQ = 2048
PAST_LEN = 128

N_MIXERS = 2
N_LAYERS_A = (DEPTH + 1) // 2
N_LAYERS_B = DEPTH // 2

A_HEADS = 8
A_HEAD_DIM = 64
A_QK_WIDTH = A_HEADS * 2 * A_HEAD_DIM
A_V_WIDTH = A_HEADS * 2 * A_HEAD_DIM
Q_BLOCK = 128

B_CONFIGS = ((128, 1), (512, 4), (2048, 16))
B_GROUPS = 3
B_HEADS = 8
B_HEAD_DIM = 64
B_GROUP_WIDTH = B_HEADS * B_HEAD_DIM
B_PROJ_WIDTH = B_GROUPS * B_GROUP_WIDTH

NUM_BUCKETS = 32
MAX_DISTANCE = 1024
N_BIAS_COLS = A_HEADS + B_GROUPS * B_HEADS

N_GROUPS = 4
EXPERTS_PER_GROUP = 8
N_EXPERTS = N_GROUPS * EXPERTS_PER_GROUP
TOP_K = 2
D_EXPERT = 512
MOE_BLOCK = 128

ALPHA = (2.0 * DEPTH) ** 0.25
BETA = (8.0 * DEPTH) ** -0.25
LN_EPS = 1e-5
NEG_INF = -1e30

kernel_name = "hybrid_diffattn_dilated_hmoe_encoder"


def rel_bucket(rel):
    nb = NUM_BUCKETS // 2
    max_exact = nb // 2
    n = jnp.abs(rel)
    side = jnp.where(rel > 0, nb, 0)
    nf = jnp.maximum(n, 1).astype(jnp.float32)
    large = max_exact + (jnp.log(nf / max_exact) / math.log(MAX_DISTANCE / max_exact)
                         * (nb - max_exact)).astype(jnp.int32)
    large = jnp.minimum(large, nb - 1)
    return side + jnp.where(n < max_exact, n, large)


def layer_norm(x, g, b):
    xf = x.astype(jnp.float32)
    mu = jnp.mean(xf, axis=-1, keepdims=True)
    var = jnp.mean(jnp.square(xf - mu), axis=-1, keepdims=True)
    return ((xf - mu) * lax.rsqrt(var + LN_EPS) * g.astype(jnp.float32)
            + b.astype(jnp.float32)).astype(x.dtype)


def diff_attention(x, w_in, w_out, lq1, lk1, lq2, lk2, sub_g, bias_cols, lambda_init):
    B, S, _ = x.shape
    qkv = jnp.einsum('bsd,de->bse', x, w_in)
    q = qkv[..., :A_QK_WIDTH].reshape(B, S, A_HEADS, 2, A_HEAD_DIM)
    k = qkv[..., A_QK_WIDTH:2 * A_QK_WIDTH].reshape(B, S, A_HEADS, 2, A_HEAD_DIM)
    v = qkv[..., 2 * A_QK_WIDTH:].reshape(B, S, A_HEADS, 2 * A_HEAD_DIM)
    lam = (jnp.exp(jnp.sum(lq1.astype(jnp.float32) * lk1.astype(jnp.float32)))
           - jnp.exp(jnp.sum(lq2.astype(jnp.float32) * lk2.astype(jnp.float32)))
           + lambda_init)
    scale = A_HEAD_DIM ** -0.5
    n_blk = S // Q_BLOCK
    q_blocks = q.reshape(B, n_blk, Q_BLOCK, A_HEADS, 2, A_HEAD_DIM).transpose(1, 0, 2, 3, 4, 5)
    k_pos = jnp.arange(S, dtype=jnp.int32)

    def one_block(args):
        qb, q_start = args
        logits = jnp.einsum('bqhcd,bkhcd->bchqk', qb, k).astype(jnp.float32) * scale
        rel = k_pos[None, :] - (q_start + jnp.arange(Q_BLOCK, dtype=jnp.int32))[:, None]
        bias = bias_cols[:, rel_bucket(rel)].astype(jnp.float32)
        p = jax.nn.softmax(logits + bias[None, None], axis=-1)
        attn = (p[:, 0] - lam * p[:, 1]).astype(v.dtype)
        return jnp.einsum('bhqk,bkhe->bqhe', attn, v)

    o = lax.map(one_block, (q_blocks, jnp.arange(n_blk, dtype=jnp.int32) * Q_BLOCK))
    of = o.transpose(1, 0, 2, 3, 4).reshape(B, S, A_HEADS, 2 * A_HEAD_DIM).astype(jnp.float32)
    of = of * lax.rsqrt(jnp.mean(jnp.square(of), axis=-1, keepdims=True) + LN_EPS)
    of = of * sub_g.astype(jnp.float32) * (1.0 - lambda_init)
    return jnp.einsum('bse,ed->bsd', of.astype(x.dtype).reshape(B, S, A_V_WIDTH), w_out)


def dilated_group(q, k, v, dilation, radius, bias_cols):
    B, S, H, dh = q.shape
    L = S // dilation
    n_blk = -(-L // radius)
    Lp = n_blk * radius

    def to_sub(t):
        t = t.reshape(B, L, dilation, H, dh).transpose(0, 2, 1, 3, 4)
        return jnp.pad(t, ((0, 0), (0, 0), (0, Lp - L), (0, 0), (0, 0)))

    def windows(t):
        t = jnp.pad(to_sub(t), ((0, 0), (0, 0), (radius, radius), (0, 0), (0, 0)))
        t = t.reshape(B, dilation, n_blk + 2, radius, H, dh)
        return jnp.concatenate([t[:, :, :-2], t[:, :, 1:-1], t[:, :, 2:]], axis=3)

    qs = to_sub(q).reshape(B, dilation, n_blk, radius, H, dh)
    kw = windows(k)
    vw = windows(v)
    logits = jnp.einsum('bdnqhe,bdnkhe->bdnhqk', qs, kw).astype(jnp.float32) * (dh ** -0.5)
    i = jnp.arange(radius, dtype=jnp.int32)[:, None]
    j = jnp.arange(3 * radius, dtype=jnp.int32)[None, :]
    rel_sub = j - radius - i
    bias = bias_cols[:, rel_bucket(rel_sub * dilation)].astype(jnp.float32)
    key_idx = jnp.arange(n_blk, dtype=jnp.int32)[:, None, None] * radius - radius + j[None]
    valid = (jnp.abs(rel_sub)[None] <= radius) & (key_idx >= 0) & (key_idx < L)
    logits = jnp.where(valid[None, None, :, None], logits + bias[None, None, None], NEG_INF)
    lse = jax.nn.logsumexp(logits, axis=-1)
    p = jnp.exp(logits - lse[..., None]).astype(v.dtype)
    o = jnp.einsum('bdnhqk,bdnkhe->bdnqhe', p, vw)
    o = o.reshape(B, dilation, Lp, H, dh)[:, :, :L].transpose(0, 2, 1, 3, 4).reshape(B, S, H, dh)
    lse = lse.transpose(0, 1, 2, 4, 3).reshape(B, dilation, Lp, H)[:, :, :L]
    lse = lse.transpose(0, 2, 1, 3).reshape(B, S, H)
    return o, lse


def dilated_attention(x, w_in, w_out, rel_bias_table):
    B, S, _ = x.shape
    qkv = jnp.einsum('bsd,de->bse', x, w_in)
    q = qkv[..., :B_PROJ_WIDTH].reshape(B, S, B_GROUPS, B_HEADS, B_HEAD_DIM)
    k = qkv[..., B_PROJ_WIDTH:2 * B_PROJ_WIDTH].reshape(B, S, B_GROUPS, B_HEADS, B_HEAD_DIM)
    v = qkv[..., 2 * B_PROJ_WIDTH:].reshape(B, S, B_GROUPS, B_HEADS, B_HEAD_DIM)
    outs, lses = [], []
    for g, (window, dil) in enumerate(B_CONFIGS):
        c0 = A_HEADS + g * B_HEADS
        cols = rel_bias_table[:, c0:c0 + B_HEADS].T
        o_g, l_g = dilated_group(q[:, :, g], k[:, :, g], v[:, :, g], dil, window // (2 * dil), cols)
        outs.append(o_g)
        lses.append(l_g)
    wts = jax.nn.softmax(jnp.stack(lses, axis=2), axis=2)
    o = jnp.sum(jnp.stack(outs, axis=2).astype(jnp.float32) * wts[..., None], axis=2)
    return jnp.einsum('bse,ed->bsd', o.astype(x.dtype).reshape(B, S, B_GROUP_WIDTH), w_out)


def hierarchical_moe(x, router_group, router_expert, w1, w3, w2):
    B, S, D = x.shape
    T = B * S
    xt = x.reshape(T, D)
    gl = jnp.einsum('td,dg->tg', xt, router_group).astype(jnp.float32)
    gp = jax.nn.softmax(gl, axis=-1)
    g_idx = jnp.argmax(gl, axis=-1).astype(jnp.int32)
    g_w = jnp.take_along_axis(gp, g_idx[:, None], axis=-1)[:, 0]
    el = jnp.einsum('td,de->te', xt, router_expert).astype(jnp.float32).reshape(T, N_GROUPS, EXPERTS_PER_GROUP)
    el_g = jnp.take_along_axis(el, g_idx[:, None, None], axis=1)[:, 0]
    top_v, top_i = lax.top_k(jax.nn.softmax(el_g, axis=-1), TOP_K)
    gate = g_w[:, None] * top_v / jnp.sum(top_v, axis=-1, keepdims=True)
    eid = g_idx[:, None] * EXPERTS_PER_GROUP + top_i.astype(jnp.int32)

    A = T * TOP_K
    e = eid.reshape(A)
    tok = jnp.repeat(jnp.arange(T, dtype=jnp.int32), TOP_K)
    gw = gate.reshape(A)
    order = jnp.argsort(e)
    e_s, tok_s, gw_s = e[order], tok[order], gw[order]
    counts = jnp.bincount(e, length=N_EXPERTS)
    padded = (counts + MOE_BLOCK - 1) // MOE_BLOCK * MOE_BLOCK
    starts = jnp.cumsum(counts) - counts
    pends = jnp.cumsum(padded)
    pstarts = pends - padded
    dest = pstarts[e_s] + jnp.arange(A, dtype=jnp.int32) - starts[e_s]
    P = (A + MOE_BLOCK - 1) // MOE_BLOCK * MOE_BLOCK + N_EXPERTS * MOE_BLOCK
    n_blk = P // MOE_BLOCK
    buf = jnp.zeros((P, D), xt.dtype).at[dest].set(xt[tok_s])
    blk_exp = jnp.minimum(jnp.searchsorted(pends, jnp.arange(n_blk) * MOE_BLOCK, side='right'),
                          N_EXPERTS - 1).astype(jnp.int32)

    def run_block(args):
        xb, ex = args
        h = jax.nn.silu(xb @ w1[ex]) * (xb @ w3[ex])
        return h @ w2[ex]

    out = lax.map(run_block, (buf.reshape(n_blk, MOE_BLOCK, D), blk_exp)).reshape(P, D)
    y = out[dest] * gw_s[:, None].astype(out.dtype)
    return jnp.zeros_like(xt).at[tok_s].add(y).reshape(B, S, D)


def trunk(x, rel_bias_table, a_w_in, a_w_out, a_lambda_q1, a_lambda_k1, a_lambda_q2, a_lambda_k2,
          a_subln_g, b_w_in, b_w_out, ln_gain, ln_bias, router_group, router_expert,
          expert_w1, expert_w3, expert_w2):
    a_cols = rel_bias_table[:, :A_HEADS].T
    for i in range(DEPTH):
        li = i // N_MIXERS
        if i % N_MIXERS == 0:
            lambda_init = 0.8 - 0.6 * math.exp(-0.3 * i)
            h = diff_attention(x, a_w_in[li], a_w_out[li], a_lambda_q1[li], a_lambda_k1[li],
                               a_lambda_q2[li], a_lambda_k2[li], a_subln_g[li], a_cols, lambda_init)
        else:
            h = dilated_attention(x, b_w_in[li], b_w_out[li], rel_bias_table)
        x = layer_norm(ALPHA * x + h, ln_gain[i, 0], ln_bias[i, 0])
        f = hierarchical_moe(x, router_group[i], router_expert[i], expert_w1[i], expert_w3[i], expert_w2[i])
        x = layer_norm(ALPHA * x + f, ln_gain[i, 1], ln_bias[i, 1])
    return x


def setup_inputs(seed: int = 0) -> dict:
    key = jax.random.key(seed)
    ks = jax.random.split(key, 24)

    def nrm(k, shape, scale):
        return jax.random.normal(k, shape, jnp.float32) * scale

    D = D_MODEL
    x_prompt = nrm(ks[0], (BATCH, SEQ, D), 1.0)
    x_sample = nrm(ks[1], (DEC_BATCH, DEC_SEQ, D), 1.0)
    rel_bias_table = nrm(ks[2], (NUM_BUCKETS, N_BIAS_COLS), 0.5)
    a_w_in = jnp.concatenate([nrm(ks[3], (N_LAYERS_A, D, 2 * A_QK_WIDTH), D ** -0.5),
                              nrm(ks[4], (N_LAYERS_A, D, A_V_WIDTH), BETA * D ** -0.5)], axis=-1)
    a_w_out = nrm(ks[5], (N_LAYERS_A, A_V_WIDTH, D), BETA * A_V_WIDTH ** -0.5)
    a_lambda_q1 = nrm(ks[6], (N_LAYERS_A, A_HEAD_DIM), 0.1)
    a_lambda_k1 = nrm(ks[7], (N_LAYERS_A, A_HEAD_DIM), 0.1)
    a_lambda_q2 = nrm(ks[8], (N_LAYERS_A, A_HEAD_DIM), 0.1)
    a_lambda_k2 = nrm(ks[9], (N_LAYERS_A, A_HEAD_DIM), 0.1)
    a_subln_g = 1.0 + nrm(ks[10], (N_LAYERS_A, 2 * A_HEAD_DIM), 0.02)
    b_w_in = jnp.concatenate([nrm(ks[11], (N_LAYERS_B, D, 2 * B_PROJ_WIDTH), D ** -0.5),
                              nrm(ks[12], (N_LAYERS_B, D, B_PROJ_WIDTH), BETA * D ** -0.5)], axis=-1)
    b_w_out = nrm(ks[13], (N_LAYERS_B, B_GROUP_WIDTH, D), BETA * B_GROUP_WIDTH ** -0.5)
    ln_gain = 1.0 + nrm(ks[14], (DEPTH, 2, D), 0.02)
    ln_bias = nrm(ks[15], (DEPTH, 2, D), 0.02)
    router_group = nrm(ks[16], (DEPTH, D, N_GROUPS), D ** -0.5)
    router_expert = nrm(ks[17], (DEPTH, D, N_EXPERTS), D ** -0.5)
    expert_w1 = nrm(ks[18], (DEPTH, N_EXPERTS, D, D_EXPERT), D ** -0.5)
    expert_w3 = nrm(ks[19], (DEPTH, N_EXPERTS, D, D_EXPERT), D ** -0.5)
    expert_w2 = nrm(ks[20], (DEPTH, N_EXPERTS, D_EXPERT, D), BETA * D_EXPERT ** -0.5)
    return {"x_prompt": x_prompt, "x_sample": x_sample, "rel_bias_table": rel_bias_table,
            "a_w_in": a_w_in, "a_w_out": a_w_out, "a_lambda_q1": a_lambda_q1, "a_lambda_k1": a_lambda_k1,
            "a_lambda_q2": a_lambda_q2, "a_lambda_k2": a_lambda_k2, "a_subln_g": a_subln_g,
            "b_w_in": b_w_in, "b_w_out": b_w_out, "ln_gain": ln_gain, "ln_bias": ln_bias,
            "router_group": router_group, "router_expert": router_expert,
            "expert_w1": expert_w1, "expert_w3": expert_w3, "expert_w2": expert_w2}


def reference(x_prompt, x_sample, rel_bias_table, a_w_in, a_w_out, a_lambda_q1, a_lambda_k1,
              a_lambda_q2, a_lambda_k2, a_subln_g, b_w_in, b_w_out, ln_gain, ln_bias,
              router_group, router_expert, expert_w1, expert_w3, expert_w2):
    y_prompt = trunk(x_prompt, rel_bias_table, a_w_in, a_w_out, a_lambda_q1, a_lambda_k1, a_lambda_q2,
                     a_lambda_k2, a_subln_g, b_w_in, b_w_out, ln_gain, ln_bias, router_group,
                     router_expert, expert_w1, expert_w3, expert_w2)
    y_sample = trunk(x_sample, rel_bias_table, a_w_in, a_w_out, a_lambda_q1, a_lambda_k1, a_lambda_q2,
                     a_lambda_k2, a_subln_g, b_w_in, b_w_out, ln_gain, ln_bias, router_group,
                     router_expert, expert_w1, expert_w3, expert_w2)
    return (y_prompt, y_sample)
```

```python
import functools
import math

import jax
import jax.numpy as jnp
from jax import lax
from jax.experimental import pallas as pl
from jax.experimental.pallas import tpu as pltpu

F32 = jnp.float32
BF16 = jnp.bfloat16

D_MODEL = 1024
DEPTH = 4
N_MIXERS = 2

A_HEADS = 8
A_HEAD_DIM = 64
A_QK_WIDTH = A_HEADS * 2 * A_HEAD_DIM
A_V_DIM = 2 * A_HEAD_DIM

B_CONFIGS = ((128, 1), (512, 4), (2048, 16))
B_GROUPS = 3
B_HEADS = 8
B_HEAD_DIM = 64
B_GROUP_WIDTH = B_HEADS * B_HEAD_DIM
B_PROJ_WIDTH = B_GROUPS * B_GROUP_WIDTH

NUM_BUCKETS = 32
MAX_DISTANCE = 1024

N_GROUPS = 4
EXPERTS_PER_GROUP = 8
N_EXPERTS = N_GROUPS * EXPERTS_PER_GROUP
D_EXPERT = 512

ALPHA = (2.0 * DEPTH) ** 0.25
LN_EPS = 1e-5
NEG = -1e30
LOG2E = math.log2(math.e)

LANES = 128
MOE_ROWS = 256
BIAS_TILE = 256
BIAS_TILE_SPAN = 4
VMEM_LIMIT = 56 * 1024 * 1024


def _cparams(sem):
    return pltpu.CompilerParams(dimension_semantics=sem, vmem_limit_bytes=VMEM_LIMIT)


def _rel_bucket(rel):
    nb = NUM_BUCKETS // 2
    max_exact = nb // 2
    n = jnp.abs(rel)
    side = jnp.where(rel > 0, nb, 0)
    nf = jnp.maximum(n, 1).astype(F32)
    large = max_exact + (jnp.log(nf / max_exact) / math.log(MAX_DISTANCE / max_exact)
                         * (nb - max_exact)).astype(jnp.int32)
    large = jnp.minimum(large, nb - 1)
    return side + jnp.where(n < max_exact, n, large)


def _proj_kernel(x_ref, w_ref, s_ref, o_ref):
    acc = jnp.dot(x_ref[...].astype(BF16), w_ref[...], preferred_element_type=F32)
    o_ref[...] = (acc * s_ref[...]).astype(o_ref.dtype)


def _project(x, w, col_scale, tm=1024, tn=512):
    m, k = x.shape
    n = w.shape[1]
    return pl.pallas_call(
        _proj_kernel,
        grid=(m // tm, n // tn),
        in_specs=[pl.BlockSpec((tm, k), lambda i, j: (i, 0)),
                  pl.BlockSpec((k, tn), lambda i, j: (0, j)),
                  pl.BlockSpec((1, tn), lambda i, j: (0, j))],
        out_specs=pl.BlockSpec((tm, tn), lambda i, j: (i, j)),
        out_shape=jax.ShapeDtypeStruct((m, n), BF16),
        compiler_params=_cparams(("parallel", "arbitrary")),
    )(x, w, col_scale)


def _diff_attn_kernel(lam_ref, subg_ref, q_ref, k_ref, vt_ref, bt_ref, o_ref, acc_ref, m_ref, l_ref,
                      *, qb, kb, nk, lambda_init):
    qi = pl.program_id(2)
    q = q_ref[...]
    lane = lax.broadcasted_iota(jnp.int32, q.shape, 1)
    zero = jnp.zeros_like(q)
    wq = jnp.concatenate([jnp.where(lane < A_HEAD_DIM, q, zero),
                          jnp.where(lane >= A_HEAD_DIM, q, zero)], axis=0)
    m_ref[...] = jnp.full(m_ref.shape, NEG, F32)
    l_ref[...] = jnp.zeros(l_ref.shape, F32)
    acc_ref[...] = jnp.zeros(acc_ref.shape, F32)
    tiles_per_chunk = kb // BIAS_TILE
    q_tile = qi * (qb // BIAS_TILE)

    def body(kc, carry):
        k0 = pl.multiple_of(kc * kb, kb)
        s = lax.dot_general(k_ref[pl.ds(k0, kb), :], wq, (((1,), (1,)), ((), ())),
                            preferred_element_type=F32)
        parts = []
        for a in range(tiles_per_chunk):
            e = kc * tiles_per_chunk + a - q_tile
            bt = bt_ref[jnp.clip(e, -BIAS_TILE_SPAN, BIAS_TILE_SPAN) + BIAS_TILE_SPAN]
            sa = s[a * BIAS_TILE:(a + 1) * BIAS_TILE, :]
            parts.append(sa + jnp.concatenate([bt] * (2 * qb // BIAS_TILE), axis=1))
        s = jnp.concatenate(parts, axis=0)
        m_old = m_ref[...]
        m_new = jnp.maximum(m_old, jnp.max(s, axis=0, keepdims=True))
        alpha = jnp.exp2(m_old - m_new)
        p = jnp.exp2(s - m_new)
        l_ref[...] = alpha * l_ref[...] + jnp.sum(p, axis=0, keepdims=True)
        pv = jnp.dot(vt_ref[kc], p.astype(BF16), preferred_element_type=F32)
        acc_ref[...] = acc_ref[...] * alpha + pv
        m_ref[...] = m_new
        return carry

    lax.fori_loop(0, nk, body, 0)

    lamv = lam_ref[...]
    lam = (jnp.exp(jnp.sum(lamv[0:1] * lamv[1:2], axis=1, keepdims=True))
           - jnp.exp(jnp.sum(lamv[2:3] * lamv[3:4], axis=1, keepdims=True)) + lambda_init)
    on = acc_ref[...] / l_ref[...]
    o = on[:, :qb] - lam * on[:, qb:]
    o = o * lax.rsqrt(jnp.mean(o * o, axis=0, keepdims=True) + LN_EPS)
    o = o * (subg_ref[...] * (1.0 - lambda_init))
    o_ref[...] = o.T.astype(o_ref.dtype)


def _diff_attention(qkv, bias_tiles, lam_vecs, subg_col, lambda_init, qb=256, kb=1024):
    b, s, _ = qkv.shape
    kb = min(kb, s)
    nk = s // kb
    v = qkv[:, :, 2 * A_QK_WIDTH:].reshape(b, nk, kb, A_HEADS, A_V_DIM)
    vt = v.transpose(0, 3, 1, 4, 2)
    kern = functools.partial(_diff_attn_kernel, qb=qb, kb=kb, nk=nk, lambda_init=lambda_init)
    n_tiles = 2 * BIAS_TILE_SPAN + 1
    return pl.pallas_call(
        kern,
        grid=(b, A_HEADS, s // qb),
        in_specs=[pl.BlockSpec((4, A_HEAD_DIM), lambda bi, h, qi: (0, 0)),
                  pl.BlockSpec((A_V_DIM, 1), lambda bi, h, qi: (0, 0)),
                  pl.BlockSpec((None, qb, LANES), lambda bi, h, qi: (bi, qi, h)),
                  pl.BlockSpec((None, s, LANES), lambda bi, h, qi: (bi, 0, A_HEADS + h)),
                  pl.BlockSpec((None, None, nk, A_V_DIM, kb), lambda bi, h, qi: (bi, h, 0, 0, 0)),
                  pl.BlockSpec((None, n_tiles, BIAS_TILE, BIAS_TILE), lambda bi, h, qi: (h, 0, 0, 0))],
        out_specs=pl.BlockSpec((None, qb, LANES), lambda bi, h, qi: (bi, qi, h)),
        out_shape=jax.ShapeDtypeStruct((b, s, A_HEADS * A_V_DIM), BF16),
        scratch_shapes=[pltpu.VMEM((A_V_DIM, 2 * qb), F32),
                        pltpu.VMEM((1, 2 * qb), F32),
                        pltpu.VMEM((1, 2 * qb), F32)],
        compiler_params=_cparams(("parallel", "parallel", "arbitrary")),
    )(lam_vecs, subg_col, qkv, qkv, vt, bias_tiles)


def _diff_bias_tiles(a_cols):
    t = BIAS_TILE
    e = (jnp.arange(2 * BIAS_TILE_SPAN + 1, dtype=jnp.int32) - BIAS_TILE_SPAN) * t
    j = jnp.arange(t, dtype=jnp.int32)
    rel = e[:, None, None] + j[None, :, None] - j[None, None, :]
    return a_cols[:, _rel_bucket(rel)].astype(F32) * LOG2E


def _dilated_kernel(q_ref, kp_ref, kc_ref, kn_ref, vp_ref, vc_ref, vn_ref, bias_ref, o_ref, lse_ref,
                    *, lb, radius, n_blk):
    li = pl.program_id(2)
    win = lb + 2 * radius
    kwin = jnp.concatenate([kp_ref[lb - radius:, :], kc_ref[...], kn_ref[:radius, :]], axis=0)
    vwin = jnp.concatenate([vp_ref[lb - radius:, :], vc_ref[...], vn_ref[:radius, :]], axis=0)
    jj = lax.broadcasted_iota(jnp.int32, (lb, win), 1)
    valid = jnp.logical_and(jnp.logical_or(li > 0, jj >= radius),
                            jnp.logical_or(li < n_blk - 1, jj < lb + radius))
    q = q_ref[...]
    for h in range(B_HEADS):
        cs = slice(h * B_HEAD_DIM, (h + 1) * B_HEAD_DIM)
        s = lax.dot_general(q[:, cs], kwin[:, cs], (((1,), (1,)), ((), ())),
                            preferred_element_type=F32)
        s = jnp.where(valid, s + bias_ref[h], NEG)
        m = jnp.max(s, axis=1, keepdims=True)
        p = jnp.exp2(s - m)
        l = jnp.sum(p, axis=1, keepdims=True)
        o = jnp.dot(p.astype(BF16), vwin[:, cs], preferred_element_type=F32) / l
        o_ref[:, cs] = o
        lse_ref[:, cs] = jnp.broadcast_to(m + jnp.log2(l), (lb, B_HEAD_DIM))


def _dilated_group(qkv, g, dilation, radius, cols):
    b, s, width = qkv.shape
    sub_len = s // dilation
    lb = min(256, sub_len)
    n_blk = sub_len // lb
    win = lb + 2 * radius
    n_col = width // B_GROUP_WIDTH
    x = qkv.reshape(b, sub_len, dilation * width)
    i = jnp.arange(lb, dtype=jnp.int32)[:, None]
    j = jnp.arange(win, dtype=jnp.int32)[None, :]
    rel_sub = j - radius - i
    bias = cols[:, _rel_bucket(rel_sub * dilation)].astype(F32) * LOG2E
    bias = jnp.where((jnp.abs(rel_sub) <= radius)[None], bias, NEG)

    def spec(col, shift):
        def index(bi, r, li):
            return (bi, jnp.clip(li + shift, 0, n_blk - 1), r * n_col + col)
        return pl.BlockSpec((None, lb, B_GROUP_WIDTH), index)

    kern = functools.partial(_dilated_kernel, lb=lb, radius=radius, n_blk=n_blk)
    out_spec = pl.BlockSpec((None, lb, B_GROUP_WIDTH), lambda bi, r, li: (bi, li, r))
    shape = jax.ShapeDtypeStruct((b, sub_len, dilation * B_GROUP_WIDTH), F32)
    o, lse = pl.pallas_call(
        kern,
        grid=(b, dilation, n_blk),
        in_specs=[spec(g, 0),
                  spec(B_GROUPS + g, -1), spec(B_GROUPS + g, 0), spec(B_GROUPS + g, 1),
                  spec(2 * B_GROUPS + g, -1), spec(2 * B_GROUPS + g, 0), spec(2 * B_GROUPS + g, 1),
                  pl.BlockSpec((B_HEADS, lb, win), lambda bi, r, li: (0, 0, 0))],
        out_specs=[out_spec, out_spec],
        out_shape=[shape, shape],
        compiler_params=_cparams(("parallel", "parallel", "arbitrary")),
    )(x, x, x, x, x, x, x, bias)
    return o.reshape(b, s, B_GROUP_WIDTH), lse.reshape(b, s, B_GROUP_WIDTH)


def _layer_norm(z, g, b):
    mu = jnp.mean(z, axis=-1, keepdims=True)
    zc = z - mu
    var = jnp.mean(zc * zc, axis=-1, keepdims=True)
    return zc * lax.rsqrt(var + LN_EPS) * g + b


def _route(logits):
    lane = lax.broadcasted_iota(jnp.int32, logits.shape, 1).astype(F32)
    far = float(LANES)
    is_group = lane < N_GROUPS
    gl = jnp.where(is_group, logits, NEG)
    gmax = jnp.max(gl, axis=1, keepdims=True)
    gidx = jnp.min(jnp.where(gl == gmax, lane, far), axis=1, keepdims=True)
    gden = jnp.sum(jnp.where(is_group, jnp.exp(gl - gmax), 0.0), axis=1, keepdims=True)
    g_w = 1.0 / gden
    lo = N_GROUPS + EXPERTS_PER_GROUP * gidx
    in_group = jnp.logical_and(lane >= lo, lane < lo + EXPERTS_PER_GROUP)
    el = jnp.where(in_group, logits, NEG)
    v1 = jnp.max(el, axis=1, keepdims=True)
    i1 = jnp.min(jnp.where(el == v1, lane, far), axis=1, keepdims=True)
    el2 = jnp.where(lane == i1, NEG, el)
    v2 = jnp.max(el2, axis=1, keepdims=True)
    i2 = jnp.min(jnp.where(el2 == v2, lane, far), axis=1, keepdims=True)
    e2 = jnp.exp(v2 - v1)
    den = 1.0 + e2
    gate0 = g_w / den
    gate1 = g_w * e2 / den
    out = jnp.where(lane == 0.0, gate0, 0.0)
    out = jnp.where(lane == 1.0, gate1, out)
    out = jnp.where(lane == 2.0, i1 - N_GROUPS, out)
    out = jnp.where(lane == 3.0, i2 - N_GROUPS, out)
    return out


def _post_mixer(x_ref, y, w_ref, g_ref, b_ref, wr_ref, x1_ref, x1b_ref, route_ref):
    h = jnp.dot(y, w_ref[...], preferred_element_type=F32)
    x1 = _layer_norm(ALPHA * x_ref[...] + h, g_ref[...], b_ref[...])
    x1b = x1.astype(BF16)
    x1_ref[...] = x1
    x1b_ref[...] = x1b
    route_ref[...] = _route(jnp.dot(x1b, wr_ref[...], preferred_element_type=F32))


def _post_diff_kernel(x_ref, o_ref, w_ref, g_ref, b_ref, wr_ref, x1_ref, x1b_ref, route_ref):
    _post_mixer(x_ref, o_ref[...], w_ref, g_ref, b_ref, wr_ref, x1_ref, x1b_ref, route_ref)


def _post_dilated_kernel(x_ref, o0_ref, o1_ref, o2_ref, l0_ref, l1_ref, l2_ref, w_ref, g_ref, b_ref, wr_ref,
                         x1_ref, x1b_ref, route_ref):
    l0, l1, l2 = l0_ref[...], l1_ref[...], l2_ref[...]
    m = jnp.maximum(jnp.maximum(l0, l1), l2)
    e0, e1, e2 = jnp.exp2(l0 - m), jnp.exp2(l1 - m), jnp.exp2(l2 - m)
    den = e0 + e1 + e2
    o = (o0_ref[...] * (e0 / den) + o1_ref[...] * (e1 / den)) + o2_ref[...] * (e2 / den)
    _post_mixer(x_ref, o.astype(BF16), w_ref, g_ref, b_ref, wr_ref, x1_ref, x1b_ref, route_ref)


def _post_mixer_call(kern, x, mixer_outs, w_out, gain, bias, w_route, tm=512):
    t, d = x.shape
    row = lambda width: pl.BlockSpec((tm, width), lambda i: (i, 0))
    whole = lambda a: pl.BlockSpec(a.shape, lambda i: (0,) * a.ndim)
    return pl.pallas_call(
        kern,
        grid=(t // tm,),
        in_specs=[row(d)] + [row(a.shape[1]) for a in mixer_outs]
        + [whole(w_out), whole(gain), whole(bias), whole(w_route)],
        out_specs=[row(d), row(d), row(LANES)],
        out_shape=[jax.ShapeDtypeStruct((t, d), F32), jax.ShapeDtypeStruct((t, d), BF16),
                   jax.ShapeDtypeStruct((t, LANES), F32)],
        compiler_params=_cparams(("parallel",)),
    )(x, *mixer_outs, w_out, gain, bias, w_route)


def _moe_mlp_kernel(blk_ref, x_ref, gw_ref, w1_ref, w3_ref, w2_ref, y_ref):
    x = x_ref[...]
    h1 = jnp.dot(x, w1_ref[...], preferred_element_type=F32)
    h3 = jnp.dot(x, w3_ref[...], preferred_element_type=F32)
    h = (h1 * jax.nn.sigmoid(h1)) * h3
    y = jnp.dot(h.astype(BF16), w2_ref[...], preferred_element_type=F32)
    y_ref[...] = y * gw_ref[...]


def _moe_mlp(xs, gw, blk_exp, w1, w3, w2):
    p, d = xs.shape
    n_blk = p // MOE_ROWS
    grid_spec = pltpu.PrefetchScalarGridSpec(
        num_scalar_prefetch=1,
        grid=(n_blk,),
        in_specs=[pl.BlockSpec((MOE_ROWS, d), lambda i, be: (i, 0)),
                  pl.BlockSpec((MOE_ROWS, 1), lambda i, be: (i, 0)),
                  pl.BlockSpec((None, d, D_EXPERT), lambda i, be: (be[i], 0, 0)),
                  pl.BlockSpec((None, d, D_EXPERT), lambda i, be: (be[i], 0, 0)),
                  pl.BlockSpec((None, D_EXPERT, d), lambda i, be: (be[i], 0, 0))],
        out_specs=pl.BlockSpec((MOE_ROWS, d), lambda i, be: (i, 0)),
    )
    return pl.pallas_call(
        _moe_mlp_kernel,
        grid_spec=grid_spec,
        out_shape=jax.ShapeDtypeStruct((p, d), F32),
        compiler_params=_cparams(("arbitrary",)),
    )(blk_exp, xs, gw, w1, w3, w2)


def _combine_ln_kernel(x_ref, ya_ref, yb_ref, g_ref, b_ref, o_ref):
    z = ALPHA * x_ref[...] + (ya_ref[...] + yb_ref[...])
    o_ref[...] = _layer_norm(z, g_ref[...], b_ref[...])


def _combine_ln(x1, ya, yb, gain, bias, tm=1024):
    t, d = x1.shape
    row = pl.BlockSpec((tm, d), lambda i: (i, 0))
    vec = pl.BlockSpec((1, d), lambda i: (0, 0))
    return pl.pallas_call(
        _combine_ln_kernel,
        grid=(t // tm,),
        in_specs=[row, row, row, vec, vec],
        out_specs=row,
        out_shape=jax.ShapeDtypeStruct((t, d), F32),
        compiler_params=_cparams(("parallel",)),
    )(x1, ya, yb, gain, bias)


def _moe(x1, x1b, route, w1, w3, w2, gain, bias):
    t = x1.shape[0]
    n_assign = 2 * t
    gate = route[:, 0:2].reshape(n_assign)
    e = route[:, 2:4].astype(jnp.int32).reshape(n_assign)
    order = jnp.argsort(e)
    e_s = e[order]
    counts = jnp.bincount(e, length=N_EXPERTS).astype(jnp.int32)
    padded = (counts + MOE_ROWS - 1) // MOE_ROWS * MOE_ROWS
    starts = jnp.cumsum(counts) - counts
    pends = jnp.cumsum(padded)
    pstarts = pends - padded
    dest = pstarts[e_s] + jnp.arange(n_assign, dtype=jnp.int32) - starts[e_s]
    n_rows = n_assign + N_EXPERTS * MOE_ROWS
    n_blk = n_rows // MOE_ROWS
    src_tok = jnp.zeros((n_rows,), jnp.int32).at[dest].set((order // 2).astype(jnp.int32))
    gw = jnp.zeros((n_rows,), F32).at[dest].set(gate[order])
    blk_exp = jnp.minimum(jnp.searchsorted(pends, jnp.arange(n_blk, dtype=jnp.int32) * MOE_ROWS, side='right'),
                          N_EXPERTS - 1).astype(jnp.int32)
    xs = jnp.take(x1b, src_tok, axis=0)
    y = _moe_mlp(xs, gw[:, None], blk_exp, w1, w3, w2)
    pos = jnp.zeros((n_assign,), jnp.int32).at[order].set(dest).reshape(t, 2)
    ya = jnp.take(y, pos[:, 0], axis=0)
    yb = jnp.take(y, pos[:, 1], axis=0)
    return _combine_ln(x1, ya, yb, gain, bias)


def _col_scale(width, n_scaled, scale):
    return jnp.where(jnp.arange(width) < n_scaled, scale, 1.0).astype(F32)[None, :]


def _trunk(x, batch_shapes, rel_bias_table, a_w_in, a_w_out, a_lambda_q1, a_lambda_k1, a_lambda_q2, a_lambda_k2,
           a_subln_g, b_w_in, b_w_out, ln_gain, ln_bias, router_group, router_expert,
           expert_w1, expert_w3, expert_w2):
    a_cols = rel_bias_table[:, :A_HEADS].T
    bias_tiles = _diff_bias_tiles(a_cols)
    a_scale = _col_scale(3 * A_QK_WIDTH, A_QK_WIDTH, A_HEAD_DIM ** -0.5 * LOG2E)
    b_scale = _col_scale(3 * B_PROJ_WIDTH, B_PROJ_WIDTH, B_HEAD_DIM ** -0.5 * LOG2E)
    offsets = []
    start = 0
    for (b, s) in batch_shapes:
        offsets.append((start, b, s))
        start += b * s

    for i in range(DEPTH):
        li = i // N_MIXERS
        w_route = jnp.concatenate(
            [router_group[i], router_expert[i],
             jnp.zeros((D_MODEL, LANES - N_GROUPS - N_EXPERTS), F32)], axis=1).astype(BF16)
        gain1, bias1 = ln_gain[i, 0][None, :], ln_bias[i, 0][None, :]
        gain2, bias2 = ln_gain[i, 1][None, :], ln_bias[i, 1][None, :]
        if i % N_MIXERS == 0:
            lambda_init = 0.8 - 0.6 * math.exp(-0.3 * i)
            qkv = _project(x, a_w_in[li].astype(BF16), a_scale)
            lam_vecs = jnp.stack([a_lambda_q1[li], a_lambda_k1[li], a_lambda_q2[li], a_lambda_k2[li]]).astype(F32)
            subg_col = a_subln_g[li].astype(F32)[:, None]
            outs = []
            for (st, b, s) in offsets:
                o = _diff_attention(qkv[st:st + b * s].reshape(b, s, -1), bias_tiles, lam_vecs, subg_col,
                                    lambda_init)
                outs.append(o.reshape(b * s, -1))
            mixer_outs = [jnp.concatenate(outs, axis=0)]
            kern = _post_diff_kernel
            w_out = a_w_out[li].astype(BF16)
        else:
            qkv = _project(x, b_w_in[li].astype(BF16), b_scale)
            per_group = []
            for g, (window, dil) in enumerate(B_CONFIGS):
                c0 = A_HEADS + g * B_HEADS
                cols = rel_bias_table[:, c0:c0 + B_HEADS].T
                os_, ls_ = [], []
                for (st, b, s) in offsets:
                    o, lse = _dilated_group(qkv[st:st + b * s].reshape(b, s, -1), g, dil, window // (2 * dil), cols)
                    os_.append(o.reshape(b * s, -1))
                    ls_.append(lse.reshape(b * s, -1))
                per_group.append((jnp.concatenate(os_, axis=0), jnp.concatenate(ls_, axis=0)))
            mixer_outs = [pg[0] for pg in per_group] + [pg[1] for pg in per_group]
            kern = _post_dilated_kernel
            w_out = b_w_out[li].astype(BF16)
        x1, x1b, route = _post_mixer_call(kern, x, mixer_outs, w_out, gain1, bias1, w_route)
        x = _moe(x1, x1b, route, expert_w1[i].astype(BF16), expert_w3[i].astype(BF16),
                 expert_w2[i].astype(BF16), gain2, bias2)
    return x


def kernel(x_prompt, x_sample, rel_bias_table, a_w_in, a_w_out, a_lambda_q1, a_lambda_k1, a_lambda_q2, a_lambda_k2, a_subln_g, b_w_in, b_w_out, ln_gain, ln_bias, router_group, router_expert, expert_w1, expert_w3, expert_w2):
    shapes = [x_prompt.shape[:2], x_sample.shape[:2]]
    x = jnp.concatenate([x_prompt.reshape(-1, D_MODEL), x_sample.reshape(-1, D_MODEL)], axis=0)
    y = _trunk(x, shapes, rel_bias_table, a_w_in, a_w_out, a_lambda_q1, a_lambda_k1, a_lambda_q2, a_lambda_k2,
               a_subln_g, b_w_in, b_w_out, ln_gain, ln_bias, router_group, router_expert,
               expert_w1, expert_w3, expert_w2)
    n_prompt = x_prompt.shape[0] * x_prompt.shape[1]
    return (y[:n_prompt].reshape(x_prompt.shape), y[n_prompt:].reshape(x_sample.shape))
```

```python
import functools
import math

import jax
import jax.numpy as jnp
from jax import lax
from jax.experimental import pallas as pl
from jax.experimental.pallas import tpu as pltpu

F32 = jnp.float32
BF16 = jnp.bfloat16

D_MODEL = 1024
DEPTH = 4
N_MIXERS = 2

A_HEADS = 8
A_HEAD_DIM = 64
A_QK_WIDTH = A_HEADS * 2 * A_HEAD_DIM
A_V_DIM = 2 * A_HEAD_DIM

B_CONFIGS = ((128, 1), (512, 4), (2048, 16))
B_GROUPS = 3
B_HEADS = 8
B_HEAD_DIM = 64
B_GROUP_WIDTH = B_HEADS * B_HEAD_DIM
B_PROJ_WIDTH = B_GROUPS * B_GROUP_WIDTH

NUM_BUCKETS = 32
MAX_DISTANCE = 1024

N_GROUPS = 4
EXPERTS_PER_GROUP = 8
N_EXPERTS = N_GROUPS * EXPERTS_PER_GROUP
D_EXPERT = 512

ALPHA = (2.0 * DEPTH) ** 0.25
LN_EPS = 1e-5
NEG = -1e30
LOG2E = math.log2(math.e)

LANES = 128
MOE_ROWS = 256
BIAS_TILE = 256
BIAS_TILE_SPAN = 4
VMEM_LIMIT = 56 * 1024 * 1024


def _cparams(sem):
    return pltpu.CompilerParams(dimension_semantics=sem, vmem_limit_bytes=VMEM_LIMIT)


def _rel_bucket(rel):
    nb = NUM_BUCKETS // 2
    max_exact = nb // 2
    n = jnp.abs(rel)
    side = jnp.where(rel > 0, nb, 0)
    nf = jnp.maximum(n, 1).astype(F32)
    large = max_exact + (jnp.log(nf / max_exact) / math.log(MAX_DISTANCE / max_exact)
                         * (nb - max_exact)).astype(jnp.int32)
    large = jnp.minimum(large, nb - 1)
    return side + jnp.where(n < max_exact, n, large)


def _bias_lookup(cols, rel):
    bucket = _rel_bucket(rel)[None]
    shape = (cols.shape[0],) + (1,) * rel.ndim
    out = jnp.zeros((cols.shape[0],) + rel.shape, F32)
    for b in range(NUM_BUCKETS):
        out = jnp.where(bucket == b, cols[:, b].astype(F32).reshape(shape), out)
    return out


def _proj_kernel(x_ref, w_ref, s_ref, o_ref):
    acc = jnp.dot(x_ref[...].astype(BF16), w_ref[...], preferred_element_type=F32)
    o_ref[...] = (acc * s_ref[...]).astype(o_ref.dtype)


def _project(x, w, col_scale, tm=1024, tn=512):
    m, k = x.shape
    n = w.shape[1]
    return pl.pallas_call(
        _proj_kernel,
        grid=(m // tm, n // tn),
        in_specs=[pl.BlockSpec((tm, k), lambda i, j: (i, 0)),
                  pl.BlockSpec((k, tn), lambda i, j: (0, j)),
                  pl.BlockSpec((1, tn), lambda i, j: (0, j))],
        out_specs=pl.BlockSpec((tm, tn), lambda i, j: (i, j)),
        out_shape=jax.ShapeDtypeStruct((m, n), BF16),
        compiler_params=_cparams(("parallel", "arbitrary")),
    )(x, w, col_scale)


def _diff_attn_kernel(lam_ref, subg_ref, q_ref, k_ref, vt_ref, bt_ref, o_ref, wq_ref, s_ref, acc_ref, m_ref, l_ref,
                      *, qb, kb, nk, lambda_init):
    qi = pl.program_id(2)
    q = q_ref[...]
    lane = lax.broadcasted_iota(jnp.int32, q.shape, 1)
    zero = jnp.zeros_like(q)
    wq_ref[...] = jnp.concatenate([jnp.where(lane < A_HEAD_DIM, q, zero),
                                   jnp.where(lane >= A_HEAD_DIM, q, zero)], axis=0)
    m_ref[...] = jnp.full(m_ref.shape, NEG, F32)
    l_ref[...] = jnp.zeros(l_ref.shape, F32)
    acc_ref[...] = jnp.zeros(acc_ref.shape, F32)
    tiles_per_chunk = kb // BIAS_TILE
    q_tile = qi * (qb // BIAS_TILE)

    def scores(kc, slot):
        k0 = pl.multiple_of(kc * kb, kb)
        s = lax.dot_general(k_ref[pl.ds(k0, kb), :], wq_ref[...], (((1,), (1,)), ((), ())),
                            preferred_element_type=F32)
        mloc = None
        for a in range(tiles_per_chunk):
            e = kc * tiles_per_chunk + a - q_tile
            bt = bt_ref[jnp.clip(e, -BIAS_TILE_SPAN, BIAS_TILE_SPAN) + BIAS_TILE_SPAN]
            rows = slice(a * BIAS_TILE, (a + 1) * BIAS_TILE)
            sa = s[rows, :] + jnp.concatenate([bt] * (2 * qb // BIAS_TILE), axis=1)
            s_ref[slot, rows, :] = sa
            ma = jnp.max(sa, axis=0, keepdims=True)
            mloc = ma if mloc is None else jnp.maximum(mloc, ma)
        return mloc

    def update(kc, slot, mloc):
        m_old = m_ref[...]
        m_new = jnp.maximum(m_old, mloc)
        alpha = jnp.exp2(m_old - m_new)
        p = jnp.exp2(s_ref[slot] - m_new)
        l_ref[...] = alpha * l_ref[...] + jnp.sum(p, axis=0, keepdims=True)
        pv = jnp.dot(vt_ref[kc], p.astype(BF16), preferred_element_type=F32)
        acc_ref[...] = acc_ref[...] * alpha + pv
        m_ref[...] = m_new

    def pair(j, mloc_even):
        mloc_odd = scores(2 * j + 1, 1)
        update(2 * j, 0, mloc_even)
        mloc_next = scores(2 * j + 2, 0)
        update(2 * j + 1, 1, mloc_odd)
        return mloc_next

    mloc_even = lax.fori_loop(0, nk // 2 - 1, pair, scores(0, 0))
    mloc_odd = scores(nk - 1, 1)
    update(nk - 2, 0, mloc_even)
    update(nk - 1, 1, mloc_odd)

    lamv = lam_ref[...]
    lam = (jnp.exp(jnp.sum(lamv[0:1] * lamv[1:2], axis=1, keepdims=True))
           - jnp.exp(jnp.sum(lamv[2:3] * lamv[3:4], axis=1, keepdims=True)) + lambda_init)
    on = acc_ref[...] / l_ref[...]
    o = on[:, :qb] - lam * on[:, qb:]
    o = o * lax.rsqrt(jnp.mean(o * o, axis=0, keepdims=True) + LN_EPS)
    o = o * (subg_ref[...] * (1.0 - lambda_init))
    o_ref[...] = o.T.astype(o_ref.dtype)


def _diff_attention(qkv, bias_tiles, lam_vecs, subg_col, lambda_init, qb=256, kb=1024):
    b, s, _ = qkv.shape
    kb = min(kb, s // 4)
    nk = s // kb
    assert qb == BIAS_TILE and kb % BIAS_TILE == 0 and nk % 2 == 0 and nk * kb == s
    v = qkv[:, :, 2 * A_QK_WIDTH:].reshape(b, nk, kb, A_HEADS, A_V_DIM)
    vt = v.transpose(0, 3, 1, 4, 2)
    kern = functools.partial(_diff_attn_kernel, qb=qb, kb=kb, nk=nk, lambda_init=lambda_init)
    n_tiles = 2 * BIAS_TILE_SPAN + 1
    return pl.pallas_call(
        kern,
        grid=(b, A_HEADS, s // qb),
        in_specs=[pl.BlockSpec((4, A_HEAD_DIM), lambda bi, h, qi: (0, 0)),
                  pl.BlockSpec((A_V_DIM, 1), lambda bi, h, qi: (0, 0)),
                  pl.BlockSpec((None, qb, LANES), lambda bi, h, qi: (bi, qi, h)),
                  pl.BlockSpec((None, s, LANES), lambda bi, h, qi: (bi, 0, A_HEADS + h)),
                  pl.BlockSpec((None, None, nk, A_V_DIM, kb), lambda bi, h, qi: (bi, h, 0, 0, 0)),
                  pl.BlockSpec((None, n_tiles, BIAS_TILE, BIAS_TILE), lambda bi, h, qi: (h, 0, 0, 0))],
        out_specs=pl.BlockSpec((None, qb, LANES), lambda bi, h, qi: (bi, qi, h)),
        out_shape=jax.ShapeDtypeStruct((b, s, A_HEADS * A_V_DIM), BF16),
        scratch_shapes=[pltpu.VMEM((2 * qb, LANES), BF16),
                        pltpu.VMEM((2, kb, 2 * qb), F32),
                        pltpu.VMEM((A_V_DIM, 2 * qb), F32),
                        pltpu.VMEM((1, 2 * qb), F32),
                        pltpu.VMEM((1, 2 * qb), F32)],
        compiler_params=_cparams(("parallel", "parallel", "arbitrary")),
    )(lam_vecs, subg_col, qkv, qkv, vt, bias_tiles)


def _diff_bias_tiles(a_cols):
    t = BIAS_TILE
    e = (jnp.arange(2 * BIAS_TILE_SPAN + 1, dtype=jnp.int32) - BIAS_TILE_SPAN) * t
    j = jnp.arange(t, dtype=jnp.int32)
    rel = e[:, None, None] + j[None, :, None] - j[None, None, :]
    return _bias_lookup(a_cols, rel) * LOG2E


def _dilated_kernel(q_ref, kp_ref, kc_ref, kn_ref, vp_ref, vc_ref, vn_ref, bias_ref, o_ref, lse_ref,
                    *, lb, radius, n_blk):
    li = pl.program_id(2)
    win = lb + 2 * radius
    kwin = jnp.concatenate([kp_ref[lb - radius:, :], kc_ref[...], kn_ref[:radius, :]], axis=0)
    vwin = jnp.concatenate([vp_ref[lb - radius:, :], vc_ref[...], vn_ref[:radius, :]], axis=0)
    jj = lax.broadcasted_iota(jnp.int32, (lb, win), 1)
    valid = jnp.logical_and(jnp.logical_or(li > 0, jj >= radius),
                            jnp.logical_or(li < n_blk - 1, jj < lb + radius))
    q = q_ref[...]
    for h in range(B_HEADS):
        cs = slice(h * B_HEAD_DIM, (h + 1) * B_HEAD_DIM)
        s = lax.dot_general(q[:, cs], kwin[:, cs], (((1,), (1,)), ((), ())),
                            preferred_element_type=F32)
        s = jnp.where(valid, s + bias_ref[h], NEG)
        m = jnp.max(s, axis=1, keepdims=True)
        p = jnp.exp2(s - m)
        l = jnp.sum(p, axis=1, keepdims=True)
        o = jnp.dot(p.astype(BF16), vwin[:, cs], preferred_element_type=F32) / l
        o_ref[:, cs] = o
        lse_ref[:, cs] = jnp.broadcast_to(m + jnp.log2(l), (lb, B_HEAD_DIM))


def _dilated_group(qkv, g, dilation, radius, cols):
    b, s, width = qkv.shape
    sub_len = s // dilation
    lb = min(256, sub_len)
    n_blk = sub_len // lb
    win = lb + 2 * radius
    n_col = width // B_GROUP_WIDTH
    x = qkv.reshape(b, sub_len, dilation * width)
    i = jnp.arange(lb, dtype=jnp.int32)[:, None]
    j = jnp.arange(win, dtype=jnp.int32)[None, :]
    rel_sub = j - radius - i
    bias = _bias_lookup(cols, rel_sub * dilation) * LOG2E
    bias = jnp.where((jnp.abs(rel_sub) <= radius)[None], bias, NEG)

    def spec(col, shift):
        def index(bi, r, li):
            return (bi, jnp.clip(li + shift, 0, n_blk - 1), r * n_col + col)
        return pl.BlockSpec((None, lb, B_GROUP_WIDTH), index)

    kern = functools.partial(_dilated_kernel, lb=lb, radius=radius, n_blk=n_blk)
    out_spec = pl.BlockSpec((None, lb, B_GROUP_WIDTH), lambda bi, r, li: (bi, li, r))
    shape = jax.ShapeDtypeStruct((b, sub_len, dilation * B_GROUP_WIDTH), F32)
    o, lse = pl.pallas_call(
        kern,
        grid=(b, dilation, n_blk),
        in_specs=[spec(g, 0),
                  spec(B_GROUPS + g, -1), spec(B_GROUPS + g, 0), spec(B_GROUPS + g, 1),
                  spec(2 * B_GROUPS + g, -1), spec(2 * B_GROUPS + g, 0), spec(2 * B_GROUPS + g, 1),
                  pl.BlockSpec((B_HEADS, lb, win), lambda bi, r, li: (0, 0, 0))],
        out_specs=[out_spec, out_spec],
        out_shape=[shape, shape],
        compiler_params=_cparams(("parallel", "parallel", "arbitrary")),
    )(x, x, x, x, x, x, x, bias)
    return o.reshape(b, s, B_GROUP_WIDTH), lse.reshape(b, s, B_GROUP_WIDTH)


def _layer_norm(z, g, b):
    mu = jnp.mean(z, axis=-1, keepdims=True)
    zc = z - mu
    var = jnp.mean(zc * zc, axis=-1, keepdims=True)
    return zc * lax.rsqrt(var + LN_EPS) * g + b


def _route(logits):
    lane = lax.broadcasted_iota(jnp.int32, logits.shape, 1).astype(F32)
    far = float(LANES)
    is_group = lane < N_GROUPS
    gl = jnp.where(is_group, logits, NEG)
    gmax = jnp.max(gl, axis=1, keepdims=True)
    gidx = jnp.min(jnp.where(gl == gmax, lane, far), axis=1, keepdims=True)
    gden = jnp.sum(jnp.where(is_group, jnp.exp(gl - gmax), 0.0), axis=1, keepdims=True)
    g_w = 1.0 / gden
    lo = N_GROUPS + EXPERTS_PER_GROUP * gidx
    in_group = jnp.logical_and(lane >= lo, lane < lo + EXPERTS_PER_GROUP)
    el = jnp.where(in_group, logits, NEG)
    v1 = jnp.max(el, axis=1, keepdims=True)
    i1 = jnp.min(jnp.where(el == v1, lane, far), axis=1, keepdims=True)
    el2 = jnp.where(lane == i1, NEG, el)
    v2 = jnp.max(el2, axis=1, keepdims=True)
    i2 = jnp.min(jnp.where(el2 == v2, lane, far), axis=1, keepdims=True)
    e2 = jnp.exp(v2 - v1)
    den = 1.0 + e2
    gate0 = g_w / den
    gate1 = g_w * e2 / den
    out = jnp.where(lane == 0.0, gate0, 0.0)
    out = jnp.where(lane == 1.0, gate1, out)
    out = jnp.where(lane == 2.0, i1 - N_GROUPS, out)
    out = jnp.where(lane == 3.0, i2 - N_GROUPS, out)
    return out


def _post_mixer(x_ref, y, w_ref, g_ref, b_ref, wr_ref, x1_ref, x1b_ref, route_ref):
    h = jnp.dot(y, w_ref[...], preferred_element_type=F32)
    x1 = _layer_norm(ALPHA * x_ref[...] + h, g_ref[...], b_ref[...])
    x1b = x1.astype(BF16)
    x1_ref[...] = x1
    x1b_ref[...] = x1b
    route_ref[...] = _route(jnp.dot(x1b, wr_ref[...], preferred_element_type=F32))


def _post_diff_kernel(x_ref, o_ref, w_ref, g_ref, b_ref, wr_ref, x1_ref, x1b_ref, route_ref):
    _post_mixer(x_ref, o_ref[...], w_ref, g_ref, b_ref, wr_ref, x1_ref, x1b_ref, route_ref)


def _post_dilated_kernel(x_ref, o0_ref, o1_ref, o2_ref, l0_ref, l1_ref, l2_ref, w_ref, g_ref, b_ref, wr_ref,
                         x1_ref, x1b_ref, route_ref):
    l0, l1, l2 = l0_ref[...], l1_ref[...], l2_ref[...]
    m = jnp.maximum(jnp.maximum(l0, l1), l2)
    e0, e1, e2 = jnp.exp2(l0 - m), jnp.exp2(l1 - m), jnp.exp2(l2 - m)
    den = e0 + e1 + e2
    o = (o0_ref[...] * (e0 / den) + o1_ref[...] * (e1 / den)) + o2_ref[...] * (e2 / den)
    _post_mixer(x_ref, o.astype(BF16), w_ref, g_ref, b_ref, wr_ref, x1_ref, x1b_ref, route_ref)


def _post_mixer_call(kern, x, mixer_outs, w_out, gain, bias, w_route, tm=512):
    t, d = x.shape
    row = lambda width: pl.BlockSpec((tm, width), lambda i: (i, 0))
    whole = lambda a: pl.BlockSpec(a.shape, lambda i: (0,) * a.ndim)
    return pl.pallas_call(
        kern,
        grid=(t // tm,),
        in_specs=[row(d)] + [row(a.shape[1]) for a in mixer_outs]
        + [whole(w_out), whole(gain), whole(bias), whole(w_route)],
        out_specs=[row(d), row(d), row(LANES)],
        out_shape=[jax.ShapeDtypeStruct((t, d), F32), jax.ShapeDtypeStruct((t, d), BF16),
                   jax.ShapeDtypeStruct((t, LANES), F32)],
        compiler_params=_cparams(("parallel",)),
    )(x, *mixer_outs, w_out, gain, bias, w_route)


def _moe_mlp_kernel(blk_ref, used_ref, x_ref, gw_ref, w1_ref, w3_ref, w2_ref, y_ref):
    i = pl.program_id(0)

    @pl.when(i < used_ref[0])
    def _():
        x = x_ref[...]
        h1 = jnp.dot(x, w1_ref[...], preferred_element_type=F32)
        h3 = jnp.dot(x, w3_ref[...], preferred_element_type=F32)
        h = (h1 * jax.nn.sigmoid(h1)) * h3
        y = jnp.dot(h.astype(BF16), w2_ref[...], preferred_element_type=F32)
        y_ref[...] = y * gw_ref[...]

    @pl.when(i >= used_ref[0])
    def _():
        y_ref[...] = jnp.zeros(y_ref.shape, y_ref.dtype)


def _moe_mlp(xs, gw, blk_exp, n_used, w1, w3, w2):
    p, d = xs.shape
    n_blk = p // MOE_ROWS
    grid_spec = pltpu.PrefetchScalarGridSpec(
        num_scalar_prefetch=2,
        grid=(n_blk,),
        in_specs=[pl.BlockSpec((MOE_ROWS, d), lambda i, be, nu: (i, 0)),
                  pl.BlockSpec((MOE_ROWS, 1), lambda i, be, nu: (i, 0)),
                  pl.BlockSpec((None, d, D_EXPERT), lambda i, be, nu: (be[i], 0, 0)),
                  pl.BlockSpec((None, d, D_EXPERT), lambda i, be, nu: (be[i], 0, 0)),
                  pl.BlockSpec((None, D_EXPERT, d), lambda i, be, nu: (be[i], 0, 0))],
        out_specs=pl.BlockSpec((MOE_ROWS, d), lambda i, be, nu: (i, 0)),
    )
    return pl.pallas_call(
        _moe_mlp_kernel,
        grid_spec=grid_spec,
        out_shape=jax.ShapeDtypeStruct((p, d), F32),
        compiler_params=_cparams(("arbitrary",)),
    )(blk_exp, n_used, xs, gw, w1, w3, w2)


def _combine_ln_kernel(x_ref, ya_ref, yb_ref, g_ref, b_ref, o_ref):
    z = ALPHA * x_ref[...] + (ya_ref[...] + yb_ref[...])
    o_ref[...] = _layer_norm(z, g_ref[...], b_ref[...])


def _combine_ln(x1, ya, yb, gain, bias, tm=1024):
    t, d = x1.shape
    row = pl.BlockSpec((tm, d), lambda i: (i, 0))
    vec = pl.BlockSpec((1, d), lambda i: (0, 0))
    return pl.pallas_call(
        _combine_ln_kernel,
        grid=(t // tm,),
        in_specs=[row, row, row, vec, vec],
        out_specs=row,
        out_shape=jax.ShapeDtypeStruct((t, d), F32),
        compiler_params=_cparams(("parallel",)),
    )(x1, ya, yb, gain, bias)


def _moe(x1, x1b, route, w1, w3, w2, gain, bias):
    t = x1.shape[0]
    n_assign = 2 * t
    take = lambda a, idx: a.at[idx].get(mode='promise_in_bounds')
    gate = route[:, 0:2].reshape(n_assign)
    e = route[:, 2:4].astype(jnp.int32).reshape(n_assign)
    order = jnp.argsort(e).astype(jnp.int32)
    rank = jnp.argsort(order).astype(jnp.int32)
    experts = jnp.arange(N_EXPERTS, dtype=jnp.int32)
    counts = jnp.sum((e[:, None] == experts[None, :]).astype(jnp.int32), axis=0)
    padded = (counts + MOE_ROWS - 1) // MOE_ROWS * MOE_ROWS
    starts = jnp.cumsum(counts) - counts
    pends = jnp.cumsum(padded)
    pstarts = pends - padded
    pos = (take(pstarts - starts, e) + rank).reshape(t, 2)
    n_rows = n_assign + N_EXPERTS * MOE_ROWS
    n_blk = n_rows // MOE_ROWS
    blk_row0 = jnp.arange(n_blk, dtype=jnp.int32) * MOE_ROWS
    blk_exp = jnp.minimum(jnp.sum((pends[None, :] <= blk_row0[:, None]).astype(jnp.int32), axis=1), N_EXPERTS - 1)
    n_used = (pends[N_EXPERTS - 1] // MOE_ROWS).astype(jnp.int32)[None]
    base = jnp.repeat(take(starts - pstarts, blk_exp), MOE_ROWS)
    limit = jnp.repeat(take(starts + counts, blk_exp), MOE_ROWS)
    sidx = jnp.arange(n_rows, dtype=jnp.int32) + base
    valid = sidx < limit
    a_src = take(order, jnp.clip(sidx, 0, n_assign - 1))
    src_tok = jnp.where(valid, a_src // 2, 0)
    gw = jnp.where(valid, take(gate, a_src), 0.0)
    xs = take(x1b, src_tok)
    y = _moe_mlp(xs, gw[:, None], blk_exp, n_used, w1, w3, w2)
    ya = take(y, pos[:, 0])
    yb = take(y, pos[:, 1])
    return _combine_ln(x1, ya, yb, gain, bias)


def _col_scale(width, n_scaled, scale):
    return jnp.where(jnp.arange(width) < n_scaled, scale, 1.0).astype(F32)[None, :]


def _trunk(x, batch_shapes, rel_bias_table, a_w_in, a_w_out, a_lambda_q1, a_lambda_k1, a_lambda_q2, a_lambda_k2,
           a_subln_g, b_w_in, b_w_out, ln_gain, ln_bias, router_group, router_expert,
           expert_w1, expert_w3, expert_w2):
    a_cols = rel_bias_table[:, :A_HEADS].T
    bias_tiles = _diff_bias_tiles(a_cols)
    a_scale = _col_scale(3 * A_QK_WIDTH, A_QK_WIDTH, A_HEAD_DIM ** -0.5 * LOG2E)
    b_scale = _col_scale(3 * B_PROJ_WIDTH, B_PROJ_WIDTH, B_HEAD_DIM ** -0.5 * LOG2E)
    offsets = []
    start = 0
    for (b, s) in batch_shapes:
        offsets.append((start, b, s))
        start += b * s

    for i in range(DEPTH):
        li = i // N_MIXERS
        w_route = jnp.concatenate(
            [router_group[i], router_expert[i],
             jnp.zeros((D_MODEL, LANES - N_GROUPS - N_EXPERTS), F32)], axis=1).astype(BF16)
        gain1, bias1 = ln_gain[i, 0][None, :], ln_bias[i, 0][None, :]
        gain2, bias2 = ln_gain[i, 1][None, :], ln_bias[i, 1][None, :]
        if i % N_MIXERS == 0:
            lambda_init = 0.8 - 0.6 * math.exp(-0.3 * i)
            qkv = _project(x, a_w_in[li].astype(BF16), a_scale)
            lam_vecs = jnp.stack([a_lambda_q1[li], a_lambda_k1[li], a_lambda_q2[li], a_lambda_k2[li]]).astype(F32)
            subg_col = a_subln_g[li].astype(F32)[:, None]
            outs = []
            for (st, b, s) in offsets:
                o = _diff_attention(qkv[st:st + b * s].reshape(b, s, -1), bias_tiles, lam_vecs, subg_col,
                                    lambda_init)
                outs.append(o.reshape(b * s, -1))
            mixer_outs = [jnp.concatenate(outs, axis=0)]
            kern = _post_diff_kernel
            w_out = a_w_out[li].astype(BF16)
        else:
            qkv = _project(x, b_w_in[li].astype(BF16), b_scale)
            per_group = []
            for g, (window, dil) in enumerate(B_CONFIGS):
                c0 = A_HEADS + g * B_HEADS
                cols = rel_bias_table[:, c0:c0 + B_HEADS].T
                os_, ls_ = [], []
                for (st, b, s) in offsets:
                    o, lse = _dilated_group(qkv[st:st + b * s].reshape(b, s, -1), g, dil, window // (2 * dil), cols)
                    os_.append(o.reshape(b * s, -1))
                    ls_.append(lse.reshape(b * s, -1))
                per_group.append((jnp.concatenate(os_, axis=0), jnp.concatenate(ls_, axis=0)))
            mixer_outs = [pg[0] for pg in per_group] + [pg[1] for pg in per_group]
            kern = _post_dilated_kernel
            w_out = b_w_out[li].astype(BF16)
        x1, x1b, route = _post_mixer_call(kern, x, mixer_outs, w_out, gain1, bias1, w_route)
        x = _moe(x1, x1b, route, expert_w1[i].astype(BF16), expert_w3[i].astype(BF16),
                 expert_w2[i].astype(BF16), gain2, bias2)
    return x


def kernel(x_prompt, x_sample, rel_bias_table, a_w_in, a_w_out, a_lambda_q1, a_lambda_k1, a_lambda_q2, a_lambda_k2, a_subln_g, b_w_in, b_w_out, ln_gain, ln_bias, router_group, router_expert, expert_w1, expert_w3, expert_w2):
    shapes = [x_prompt.shape[:2], x_sample.shape[:2]]
    x = jnp.concatenate([x_prompt.reshape(-1, D_MODEL), x_sample.reshape(-1, D_MODEL)], axis=0)
    y = _trunk(x, shapes, rel_bias_table, a_w_in, a_w_out, a_lambda_q1, a_lambda_k1, a_lambda_q2, a_lambda_k2,
               a_subln_g, b_w_in, b_w_out, ln_gain, ln_bias, router_group, router_expert,
               expert_w1, expert_w3, expert_w2)
    n_prompt = x_prompt.shape[0] * x_prompt.shape[1]
    return (y[:n_prompt].reshape(x_prompt.shape), y[n_prompt:].reshape(x_sample.shape))
```

```python
import functools
import math

import jax
import jax.numpy as jnp
from jax import lax
from jax.experimental import pallas as pl
from jax.experimental.pallas import tpu as pltpu

F32 = jnp.float32
BF16 = jnp.bfloat16

D_MODEL = 1024
DEPTH = 4
N_MIXERS = 2

A_HEADS = 8
A_HEAD_DIM = 64
A_QK_WIDTH = A_HEADS * 2 * A_HEAD_DIM
A_V_DIM = 2 * A_HEAD_DIM

B_CONFIGS = ((128, 1), (512, 4), (2048, 16))
B_GROUPS = 3
B_HEADS = 8
B_HEAD_DIM = 64
B_GROUP_WIDTH = B_HEADS * B_HEAD_DIM
B_PROJ_WIDTH = B_GROUPS * B_GROUP_WIDTH

NUM_BUCKETS = 32
MAX_DISTANCE = 1024

N_GROUPS = 4
EXPERTS_PER_GROUP = 8
N_EXPERTS = N_GROUPS * EXPERTS_PER_GROUP
D_EXPERT = 512

ALPHA = (2.0 * DEPTH) ** 0.25
LN_EPS = 1e-5
NEG = -1e30
LOG2E = math.log2(math.e)

LANES = 128
BF16_SUBLANES = 16
MOE_ROWS = 256
BIAS_TILE = 256
BIAS_TILE_SPAN = 4
VT_ROWS = A_V_DIM + BF16_SUBLANES
VMEM_LIMIT = 56 * 1024 * 1024


def _cparams(sem):
    return pltpu.CompilerParams(dimension_semantics=sem, vmem_limit_bytes=VMEM_LIMIT)


def _rel_bucket(rel):
    nb = NUM_BUCKETS // 2
    max_exact = nb // 2
    n = jnp.abs(rel)
    side = jnp.where(rel > 0, nb, 0)
    nf = jnp.maximum(n, 1).astype(F32)
    large = max_exact + (jnp.log(nf / max_exact) / math.log(MAX_DISTANCE / max_exact)
                         * (nb - max_exact)).astype(jnp.int32)
    large = jnp.minimum(large, nb - 1)
    return side + jnp.where(n < max_exact, n, large)


def _bias_lookup(cols, rel):
    bucket = _rel_bucket(rel)[None]
    shape = (cols.shape[0],) + (1,) * rel.ndim
    out = jnp.zeros((cols.shape[0],) + rel.shape, F32)
    for b in range(NUM_BUCKETS):
        out = jnp.where(bucket == b, cols[:, b].astype(F32).reshape(shape), out)
    return out


def _proj_kernel(x_ref, w_ref, s_ref, o_ref):
    acc = jnp.dot(x_ref[...].astype(BF16), w_ref[...], preferred_element_type=F32)
    o_ref[...] = (acc * s_ref[...]).astype(o_ref.dtype)


def _project(x, w, col_scale, tm=1024, tn=512):
    m, k = x.shape
    n = w.shape[1]
    return pl.pallas_call(
        _proj_kernel,
        grid=(m // tm, n // tn),
        in_specs=[pl.BlockSpec((tm, k), lambda i, j: (i, 0)),
                  pl.BlockSpec((k, tn), lambda i, j: (0, j)),
                  pl.BlockSpec((1, tn), lambda i, j: (0, j))],
        out_specs=pl.BlockSpec((tm, tn), lambda i, j: (i, j)),
        out_shape=jax.ShapeDtypeStruct((m, n), BF16),
        compiler_params=_cparams(("parallel", "arbitrary")),
    )(x, w, col_scale)


def _diff_attn_kernel(lam_ref, subg_ref, q_ref, k_ref, vt_ref, bt_ref, *rest, qb, kb, nk, lambda_init, aliased):
    o_ref, wq_ref, s_ref, acc_ref, m_ref = rest[1:] if aliased else rest
    qi = pl.program_id(2)
    q = q_ref[...]
    lane = lax.broadcasted_iota(jnp.int32, q.shape, 1)
    zero = jnp.zeros_like(q)
    wq_ref[...] = jnp.concatenate([jnp.where(lane < A_HEAD_DIM, q, zero),
                                   jnp.where(lane >= A_HEAD_DIM, q, zero)], axis=0)
    m_ref[...] = jnp.full(m_ref.shape, NEG, F32)
    acc_ref[...] = jnp.zeros(acc_ref.shape, F32)
    tpc = kb // BIAS_TILE
    reps = 2 * qb // BIAS_TILE
    span = BIAS_TILE_SPAN

    jb0 = jnp.maximum((qi - (span - 1)) // tpc, 0)
    jb1 = jnp.minimum((qi + (span - 1)) // tpc + 1, nk)
    n_true_band = jb1 - jb0
    far_pairs = (nk - n_true_band) // 2
    n_band = nk - 2 * far_pairs
    leftover = jnp.where(jb1 < nk, nk - 1, jb0 - 1)
    off_left = jnp.concatenate([bt_ref[0][0:1, :]] * reps, axis=1)
    off_right = jnp.concatenate([bt_ref[2 * span][0:1, :]] * reps, axis=1)
    off_zero = jnp.zeros_like(off_left)

    def band_chunk(v):
        return jnp.where(v < n_true_band, jb0 + v, leftover)

    def far_chunk(w):
        return jnp.where(w < jb0, w, w - jb0 + jb1)

    def far_off(w):
        return jnp.where(w < jb0, off_left, off_right)

    def logits(kc):
        k0 = pl.multiple_of(kc * kb, kb)
        return lax.dot_general(k_ref[pl.ds(k0, kb), :], wq_ref[...], (((1,), (1,)), ((), ())),
                               preferred_element_type=F32)

    def scores_band(kc, slot):
        s = logits(kc)
        mloc = None
        for a in range(tpc):
            e = kc * tpc + a - qi
            bt = bt_ref[jnp.clip(e, -span, span) + span]
            rows = slice(a * BIAS_TILE, (a + 1) * BIAS_TILE)
            sa = s[rows, :] + jnp.concatenate([bt] * reps, axis=1)
            s_ref[slot, rows, :] = sa
            ma = jnp.max(sa, axis=0, keepdims=True)
            mloc = ma if mloc is None else jnp.maximum(mloc, ma)
        return mloc

    def scores_far(kc, slot, off):
        s = logits(kc)
        s_ref[slot] = s
        return jnp.max(s, axis=0, keepdims=True) + off

    def update(kc, slot, mloc, off):
        m_old = m_ref[...]
        m_new = jnp.maximum(m_old, mloc)
        alpha = jnp.exp2(m_old - m_new)
        p = jnp.exp2(s_ref[slot] - (m_new - off))
        pv = jnp.dot(vt_ref[kc], p.astype(BF16), preferred_element_type=F32)
        acc_ref[...] = acc_ref[...] * alpha + pv
        m_ref[...] = m_new

    def band_pair(j, carry):
        mloc_p, kc_p = carry
        k1, k2 = band_chunk(2 * j + 1), band_chunk(2 * j + 2)
        m1 = scores_band(k1, 1)
        update(kc_p, 0, mloc_p, off_zero)
        m2 = scores_band(k2, 0)
        update(k1, 1, m1, off_zero)
        return m2, k2

    def far_pair(j, carry):
        mloc_p, kc_p, off_p = carry
        k1, k2 = far_chunk(2 * j), far_chunk(2 * j + 1)
        o1, o2 = far_off(2 * j), far_off(2 * j + 1)
        m1 = scores_far(k1, 1, o1)
        update(kc_p, 0, mloc_p, off_p)
        m2 = scores_far(k2, 0, o2)
        update(k1, 1, m1, o1)
        return m2, k2, o2

    kc0 = band_chunk(0)
    mloc_p, kc_p = lax.fori_loop(0, n_band // 2 - 1, band_pair, (scores_band(kc0, 0), kc0))
    mloc_p, kc_p, off_p = lax.fori_loop(0, far_pairs, far_pair, (mloc_p, kc_p, off_zero))
    kc_t = band_chunk(n_band - 1)
    mloc_t = scores_band(kc_t, 1)
    update(kc_p, 0, mloc_p, off_p)
    update(kc_t, 1, mloc_t, off_zero)

    lamv = lam_ref[...]
    lam = (jnp.exp(jnp.sum(lamv[0:1] * lamv[1:2], axis=1, keepdims=True))
           - jnp.exp(jnp.sum(lamv[2:3] * lamv[3:4], axis=1, keepdims=True)) + lambda_init)
    acc = acc_ref[...]
    on = acc[:A_V_DIM] / acc[A_V_DIM:A_V_DIM + 1]
    o = on[:, :qb] - lam * on[:, qb:]
    o = o * lax.rsqrt(jnp.mean(o * o, axis=0, keepdims=True) + LN_EPS)
    o = o * (subg_ref[...] * (1.0 - lambda_init))
    o_ref[...] = o.T.astype(o_ref.dtype)


def _diff_attention(qkv, row0, b, s, prev_out, bias_tiles, lam_vecs, subg_col, lambda_init, qb=256, kb=1024):
    t = qkv.shape[0]
    kb = min(kb, s // 4)
    nk = s // kb
    assert qb == BIAS_TILE and kb % BIAS_TILE == 0 and nk % 2 == 0 and nk * kb == s
    assert row0 % s == 0 and s % qb == 0
    v = qkv[row0:row0 + b * s, 2 * A_QK_WIDTH:].reshape(b, nk, kb, A_HEADS, A_V_DIM).transpose(0, 3, 1, 4, 2)
    ones = jnp.ones((b, A_HEADS, nk, 1, kb), BF16)
    pad = jnp.zeros((b, A_HEADS, nk, VT_ROWS - A_V_DIM - 1, kb), BF16)
    vt = jnp.concatenate([v, ones, pad], axis=3)
    aliased = prev_out is not None
    kern = functools.partial(_diff_attn_kernel, qb=qb, kb=kb, nk=nk, lambda_init=lambda_init, aliased=aliased)
    n_tiles = 2 * BIAS_TILE_SPAN + 1
    q_blk0, k_blk0 = row0 // qb, row0 // s
    q_blocks = s // qb
    in_specs = [pl.BlockSpec((4, A_HEAD_DIM), lambda bi, h, qi: (0, 0)),
                pl.BlockSpec((A_V_DIM, 1), lambda bi, h, qi: (0, 0)),
                pl.BlockSpec((qb, LANES), lambda bi, h, qi: (q_blk0 + bi * q_blocks + qi, h)),
                pl.BlockSpec((s, LANES), lambda bi, h, qi: (k_blk0 + bi, A_HEADS + h)),
                pl.BlockSpec((None, None, nk, VT_ROWS, kb), lambda bi, h, qi: (bi, h, 0, 0, 0)),
                pl.BlockSpec((None, n_tiles, BIAS_TILE, BIAS_TILE), lambda bi, h, qi: (h, 0, 0, 0))]
    args = [lam_vecs, subg_col, qkv, qkv, vt, bias_tiles]
    if aliased:
        in_specs.append(pl.BlockSpec(memory_space=pl.ANY))
        args.append(prev_out)
    return pl.pallas_call(
        kern,
        grid=(b, A_HEADS, q_blocks),
        in_specs=in_specs,
        out_specs=pl.BlockSpec((qb, LANES), lambda bi, h, qi: (q_blk0 + bi * q_blocks + qi, h)),
        out_shape=jax.ShapeDtypeStruct((t, A_HEADS * A_V_DIM), BF16),
        scratch_shapes=[pltpu.VMEM((2 * qb, LANES), BF16),
                        pltpu.VMEM((2, kb, 2 * qb), F32),
                        pltpu.VMEM((VT_ROWS, 2 * qb), F32),
                        pltpu.VMEM((1, 2 * qb), F32)],
        input_output_aliases={len(args) - 1: 0} if aliased else {},
        compiler_params=_cparams(("parallel", "parallel", "arbitrary")),
    )(*args)


def _diff_bias_tiles(a_cols):
    t = BIAS_TILE
    e = (jnp.arange(2 * BIAS_TILE_SPAN + 1, dtype=jnp.int32) - BIAS_TILE_SPAN) * t
    j = jnp.arange(t, dtype=jnp.int32)
    rel = e[:, None, None] + j[None, :, None] - j[None, None, :]
    return _bias_lookup(a_cols, rel) * LOG2E


def _dilated_kernel(q_ref, kp_ref, kc_ref, kn_ref, vp_ref, vc_ref, vn_ref, bias_ref, *rest,
                    lb, radius, n_blk, aliased):
    o_ref, lse_ref = rest[2:] if aliased else rest
    li = pl.program_id(2)
    win = lb + 2 * radius
    kwin = jnp.concatenate([kp_ref[lb - radius:, :], kc_ref[...], kn_ref[:radius, :]], axis=0)
    vwin = jnp.concatenate([vp_ref[lb - radius:, :], vc_ref[...], vn_ref[:radius, :]], axis=0)
    jj = lax.broadcasted_iota(jnp.int32, (lb, win), 1)
    valid = jnp.logical_and(jnp.logical_or(li > 0, jj >= radius),
                            jnp.logical_or(li < n_blk - 1, jj < lb + radius))
    q = q_ref[...]
    for h in range(B_HEADS):
        cs = slice(h * B_HEAD_DIM, (h + 1) * B_HEAD_DIM)
        s = lax.dot_general(q[:, cs], kwin[:, cs], (((1,), (1,)), ((), ())),
                            preferred_element_type=F32)
        s = jnp.where(valid, s + bias_ref[h], NEG)
        m = jnp.max(s, axis=1, keepdims=True)
        p = jnp.exp2(s - m)
        l = jnp.sum(p, axis=1, keepdims=True)
        o = jnp.dot(p.astype(BF16), vwin[:, cs], preferred_element_type=F32) / l
        o_ref[:, cs] = o
        lse_ref[:, cs] = jnp.broadcast_to(m + jnp.log2(l), (lb, B_HEAD_DIM))


def _dilated_group(qkv, row0, b, s, prev, g, dilation, radius, cols):
    t, width = qkv.shape
    sub_len = s // dilation
    lb = min(256, sub_len)
    n_blk = sub_len // lb
    win = lb + 2 * radius
    n_col = width // B_GROUP_WIDTH
    assert row0 % (dilation * lb) == 0 and n_blk * lb == sub_len
    blk0 = row0 // dilation // lb
    x = qkv.reshape(t // dilation, dilation * width)
    i = jnp.arange(lb, dtype=jnp.int32)[:, None]
    j = jnp.arange(win, dtype=jnp.int32)[None, :]
    rel_sub = j - radius - i
    bias = _bias_lookup(cols, rel_sub * dilation) * LOG2E
    bias = jnp.where((jnp.abs(rel_sub) <= radius)[None], bias, NEG)

    def spec(col, shift):
        def index(bi, r, li):
            return (blk0 + bi * n_blk + jnp.clip(li + shift, 0, n_blk - 1), r * n_col + col)
        return pl.BlockSpec((lb, B_GROUP_WIDTH), index)

    aliased = prev is not None
    kern = functools.partial(_dilated_kernel, lb=lb, radius=radius, n_blk=n_blk, aliased=aliased)
    out_spec = pl.BlockSpec((lb, B_GROUP_WIDTH), lambda bi, r, li: (blk0 + bi * n_blk + li, r))
    shape = jax.ShapeDtypeStruct((t // dilation, dilation * B_GROUP_WIDTH), F32)
    in_specs = [spec(g, 0),
                spec(B_GROUPS + g, -1), spec(B_GROUPS + g, 0), spec(B_GROUPS + g, 1),
                spec(2 * B_GROUPS + g, -1), spec(2 * B_GROUPS + g, 0), spec(2 * B_GROUPS + g, 1),
                pl.BlockSpec((B_HEADS, lb, win), lambda bi, r, li: (0, 0, 0))]
    args = [x, x, x, x, x, x, x, bias]
    aliases = {}
    if aliased:
        in_specs += [pl.BlockSpec(memory_space=pl.ANY)] * 2
        args += list(prev)
        aliases = {len(args) - 2: 0, len(args) - 1: 1}
    return pl.pallas_call(
        kern,
        grid=(b, dilation, n_blk),
        in_specs=in_specs,
        out_specs=[out_spec, out_spec],
        out_shape=[shape, shape],
        input_output_aliases=aliases,
        compiler_params=_cparams(("parallel", "parallel", "arbitrary")),
    )(*args)


def _layer_norm(z, g, b):
    mu = jnp.mean(z, axis=-1, keepdims=True)
    zc = z - mu
    var = jnp.mean(zc * zc, axis=-1, keepdims=True)
    return zc * lax.rsqrt(var + LN_EPS) * g + b


def _route(logits):
    lane = lax.broadcasted_iota(jnp.int32, logits.shape, 1).astype(F32)
    far = float(LANES)
    is_group = lane < N_GROUPS
    gl = jnp.where(is_group, logits, NEG)
    gmax = jnp.max(gl, axis=1, keepdims=True)
    gidx = jnp.min(jnp.where(gl == gmax, lane, far), axis=1, keepdims=True)
    gden = jnp.sum(jnp.where(is_group, jnp.exp(gl - gmax), 0.0), axis=1, keepdims=True)
    g_w = 1.0 / gden
    lo = N_GROUPS + EXPERTS_PER_GROUP * gidx
    in_group = jnp.logical_and(lane >= lo, lane < lo + EXPERTS_PER_GROUP)
    el = jnp.where(in_group, logits, NEG)
    v1 = jnp.max(el, axis=1, keepdims=True)
    i1 = jnp.min(jnp.where(el == v1, lane, far), axis=1, keepdims=True)
    el2 = jnp.where(lane == i1, NEG, el)
    v2 = jnp.max(el2, axis=1, keepdims=True)
    i2 = jnp.min(jnp.where(el2 == v2, lane, far), axis=1, keepdims=True)
    e2 = jnp.exp(v2 - v1)
    den = 1.0 + e2
    gate0 = g_w / den
    gate1 = g_w * e2 / den
    out = jnp.where(lane == 0.0, gate0, 0.0)
    out = jnp.where(lane == 1.0, gate1, out)
    out = jnp.where(lane == 2.0, i1 - N_GROUPS, out)
    out = jnp.where(lane == 3.0, i2 - N_GROUPS, out)
    return out


def _post_mixer(x_ref, y, w_ref, g_ref, b_ref, wr_ref, x1_ref, x1b_ref, route_ref):
    h = jnp.dot(y, w_ref[...], preferred_element_type=F32)
    x1 = _layer_norm(ALPHA * x_ref[...] + h, g_ref[...], b_ref[...])
    x1b = x1.astype(BF16)
    x1_ref[...] = x1
    x1b_ref[...] = x1b
    route_ref[...] = _route(jnp.dot(x1b, wr_ref[...], preferred_element_type=F32))


def _post_diff_kernel(x_ref, o_ref, w_ref, g_ref, b_ref, wr_ref, x1_ref, x1b_ref, route_ref):
    _post_mixer(x_ref, o_ref[...], w_ref, g_ref, b_ref, wr_ref, x1_ref, x1b_ref, route_ref)


def _post_dilated_kernel(x_ref, o0_ref, o1_ref, o2_ref, l0_ref, l1_ref, l2_ref, w_ref, g_ref, b_ref, wr_ref,
                         x1_ref, x1b_ref, route_ref):
    l0, l1, l2 = l0_ref[...], l1_ref[...], l2_ref[...]
    m = jnp.maximum(jnp.maximum(l0, l1), l2)
    e0, e1, e2 = jnp.exp2(l0 - m), jnp.exp2(l1 - m), jnp.exp2(l2 - m)
    den = e0 + e1 + e2
    o = (o0_ref[...] * (e0 / den) + o1_ref[...] * (e1 / den)) + o2_ref[...] * (e2 / den)
    _post_mixer(x_ref, o.astype(BF16), w_ref, g_ref, b_ref, wr_ref, x1_ref, x1b_ref, route_ref)


def _post_mixer_call(kern, x, mixer_outs, w_out, gain, bias, w_route, tm=512):
    t, d = x.shape
    row = lambda width: pl.BlockSpec((tm, width), lambda i: (i, 0))
    whole = lambda a: pl.BlockSpec(a.shape, lambda i: (0,) * a.ndim)
    return pl.pallas_call(
        kern,
        grid=(t // tm,),
        in_specs=[row(d)] + [row(a.shape[1]) for a in mixer_outs]
        + [whole(w_out), whole(gain), whole(bias), whole(w_route)],
        out_specs=[row(d), row(d), row(LANES)],
        out_shape=[jax.ShapeDtypeStruct((t, d), F32), jax.ShapeDtypeStruct((t, d), BF16),
                   jax.ShapeDtypeStruct((t, LANES), F32)],
        compiler_params=_cparams(("parallel",)),
    )(x, *mixer_outs, w_out, gain, bias, w_route)


def _moe_mlp_kernel(blk_ref, used_ref, x_ref, gw_ref, w1_ref, w3_ref, w2_ref, y_ref, w1b_ref, w3b_ref, w2b_ref):
    i = pl.program_id(0)
    active = i < used_ref[0]
    new_expert = jnp.logical_or(i == 0, blk_ref[i] != blk_ref[jnp.maximum(i - 1, 0)])

    @pl.when(jnp.logical_and(active, new_expert))
    def _():
        w1b_ref[...] = w1_ref[...].astype(BF16)
        w3b_ref[...] = w3_ref[...].astype(BF16)
        w2b_ref[...] = w2_ref[...].astype(BF16)

    @pl.when(active)
    def _():
        x = x_ref[...]
        h1 = jnp.dot(x, w1b_ref[...], preferred_element_type=F32)
        h3 = jnp.dot(x, w3b_ref[...], preferred_element_type=F32)
        h = (h1 * jax.nn.sigmoid(h1)) * h3
        y = jnp.dot(h.astype(BF16), w2b_ref[...], preferred_element_type=F32)
        y_ref[...] = (y * gw_ref[...]).astype(y_ref.dtype)

    @pl.when(jnp.logical_not(active))
    def _():
        y_ref[...] = jnp.zeros(y_ref.shape, y_ref.dtype)


def _moe_mlp(xs, gw, blk_exp, n_used, layer, w1, w3, w2):
    p, d = xs.shape
    n_blk = p // MOE_ROWS
    grid_spec = pltpu.PrefetchScalarGridSpec(
        num_scalar_prefetch=2,
        grid=(n_blk,),
        in_specs=[pl.BlockSpec((MOE_ROWS, d), lambda i, be, nu: (i, 0)),
                  pl.BlockSpec((MOE_ROWS, 1), lambda i, be, nu: (i, 0)),
                  pl.BlockSpec((None, None, d, D_EXPERT), lambda i, be, nu: (layer, be[i], 0, 0)),
                  pl.BlockSpec((None, None, d, D_EXPERT), lambda i, be, nu: (layer, be[i], 0, 0)),
                  pl.BlockSpec((None, None, D_EXPERT, d), lambda i, be, nu: (layer, be[i], 0, 0))],
        out_specs=pl.BlockSpec((MOE_ROWS, d), lambda i, be, nu: (i, 0)),
        scratch_shapes=[pltpu.VMEM((d, D_EXPERT), BF16), pltpu.VMEM((d, D_EXPERT), BF16),
                        pltpu.VMEM((D_EXPERT, d), BF16)],
    )
    return pl.pallas_call(
        _moe_mlp_kernel,
        grid_spec=grid_spec,
        out_shape=jax.ShapeDtypeStruct((p, d), BF16),
        compiler_params=_cparams(("arbitrary",)),
    )(blk_exp, n_used, xs, gw, w1, w3, w2)


def _combine_ln_kernel(x_ref, ya_ref, yb_ref, g_ref, b_ref, o_ref):
    z = ALPHA * x_ref[...] + (ya_ref[...].astype(F32) + yb_ref[...].astype(F32))
    o_ref[...] = _layer_norm(z, g_ref[...], b_ref[...])


def _combine_ln(x1, ya, yb, gain, bias, tm=1024):
    t, d = x1.shape
    row = pl.BlockSpec((tm, d), lambda i: (i, 0))
    vec = pl.BlockSpec((1, d), lambda i: (0, 0))
    return pl.pallas_call(
        _combine_ln_kernel,
        grid=(t // tm,),
        in_specs=[row, row, row, vec, vec],
        out_specs=row,
        out_shape=jax.ShapeDtypeStruct((t, d), F32),
        compiler_params=_cparams(("parallel",)),
    )(x1, ya, yb, gain, bias)


def _moe(x1, x1b, route, layer, w1, w3, w2, gain, bias):
    t = x1.shape[0]
    n_assign = 2 * t
    take = lambda a, idx: a.at[idx].get(mode='promise_in_bounds')
    gate = route[:, 0:2].reshape(n_assign)
    e = route[:, 2:4].astype(jnp.int32).reshape(n_assign)
    order = jnp.argsort(e).astype(jnp.int32)
    rank = jnp.argsort(order).astype(jnp.int32)
    experts = jnp.arange(N_EXPERTS, dtype=jnp.int32)
    counts = jnp.sum((e[:, None] == experts[None, :]).astype(jnp.int32), axis=0)
    padded = (counts + MOE_ROWS - 1) // MOE_ROWS * MOE_ROWS
    starts = jnp.cumsum(counts) - counts
    pends = jnp.cumsum(padded)
    pstarts = pends - padded
    pos = (take(pstarts - starts, e) + rank).reshape(t, 2)
    n_rows = n_assign + N_EXPERTS * MOE_ROWS
    n_blk = n_rows // MOE_ROWS
    blk_row0 = jnp.arange(n_blk, dtype=jnp.int32) * MOE_ROWS
    blk_exp = jnp.minimum(jnp.sum((pends[None, :] <= blk_row0[:, None]).astype(jnp.int32), axis=1), N_EXPERTS - 1)
    n_used = (pends[N_EXPERTS - 1] // MOE_ROWS).astype(jnp.int32)[None]
    base = jnp.repeat(take(starts - pstarts, blk_exp), MOE_ROWS)
    limit = jnp.repeat(take(starts + counts, blk_exp), MOE_ROWS)
    sidx = jnp.arange(n_rows, dtype=jnp.int32) + base
    valid = sidx < limit
    a_src = take(order, jnp.clip(sidx, 0, n_assign - 1))
    src_tok = jnp.where(valid, a_src // 2, 0)
    gw = jnp.where(valid, take(gate, a_src), 0.0)
    xs = take(x1b, src_tok)
    y = _moe_mlp(xs, gw[:, None], blk_exp, n_used, layer, w1, w3, w2)
    ya = take(y, pos[:, 0])
    yb = take(y, pos[:, 1])
    return _combine_ln(x1, ya, yb, gain, bias)


def _col_scale(width, n_scaled, scale):
    return jnp.where(jnp.arange(width) < n_scaled, scale, 1.0).astype(F32)[None, :]


def _trunk(x, batch_shapes, rel_bias_table, a_w_in, a_w_out, a_lambda_q1, a_lambda_k1, a_lambda_q2, a_lambda_k2,
           a_subln_g, b_w_in, b_w_out, ln_gain, ln_bias, router_group, router_expert,
           expert_w1, expert_w3, expert_w2):
    a_cols = rel_bias_table[:, :A_HEADS].T
    bias_tiles = _diff_bias_tiles(a_cols)
    a_scale = _col_scale(3 * A_QK_WIDTH, A_QK_WIDTH, A_HEAD_DIM ** -0.5 * LOG2E)
    b_scale = _col_scale(3 * B_PROJ_WIDTH, B_PROJ_WIDTH, B_HEAD_DIM ** -0.5 * LOG2E)
    offsets = []
    start = 0
    for (b, s) in batch_shapes:
        offsets.append((start, b, s))
        start += b * s
    t = x.shape[0]

    for i in range(DEPTH):
        li = i // N_MIXERS
        w_route = jnp.concatenate(
            [router_group[i], router_expert[i],
             jnp.zeros((D_MODEL, LANES - N_GROUPS - N_EXPERTS), F32)], axis=1).astype(BF16)
        gain1, bias1 = ln_gain[i, 0][None, :], ln_bias[i, 0][None, :]
        gain2, bias2 = ln_gain[i, 1][None, :], ln_bias[i, 1][None, :]
        if i % N_MIXERS == 0:
            lambda_init = 0.8 - 0.6 * math.exp(-0.3 * i)
            qkv = _project(x, a_w_in[li].astype(BF16), a_scale)
            lam_vecs = jnp.stack([a_lambda_q1[li], a_lambda_k1[li], a_lambda_q2[li], a_lambda_k2[li]]).astype(F32)
            subg_col = a_subln_g[li].astype(F32)[:, None]
            o = None
            for (st, b, s) in offsets:
                o = _diff_attention(qkv, st, b, s, o, bias_tiles, lam_vecs, subg_col, lambda_init)
            mixer_outs = [o]
            kern = _post_diff_kernel
            w_out = a_w_out[li].astype(BF16)
        else:
            qkv = _project(x, b_w_in[li].astype(BF16), b_scale)
            outs, lses = [], []
            for g, (window, dil) in enumerate(B_CONFIGS):
                c0 = A_HEADS + g * B_HEADS
                cols = rel_bias_table[:, c0:c0 + B_HEADS].T
                pair = None
                for (st, b, s) in offsets:
                    pair = _dilated_group(qkv, st, b, s, pair, g, dil, window // (2 * dil), cols)
                outs.append(pair[0].reshape(t, B_GROUP_WIDTH))
                lses.append(pair[1].reshape(t, B_GROUP_WIDTH))
            mixer_outs = outs + lses
            kern = _post_dilated_kernel
            w_out = b_w_out[li].astype(BF16)
        x1, x1b, route = _post_mixer_call(kern, x, mixer_outs, w_out, gain1, bias1, w_route)
        x = _moe(x1, x1b, route, i, expert_w1, expert_w3, expert_w2, gain2, bias2)
    return x


def kernel(x_prompt, x_sample, rel_bias_table, a_w_in, a_w_out, a_lambda_q1, a_lambda_k1, a_lambda_q2, a_lambda_k2, a_subln_g, b_w_in, b_w_out, ln_gain, ln_bias, router_group, router_expert, expert_w1, expert_w3, expert_w2):
    shapes = [x_prompt.shape[:2], x_sample.shape[:2]]
    x = jnp.concatenate([x_prompt.reshape(-1, D_MODEL), x_sample.reshape(-1, D_MODEL)], axis=0)
    y = _trunk(x, shapes, rel_bias_table, a_w_in, a_w_out, a_lambda_q1, a_lambda_k1, a_lambda_q2, a_lambda_k2,
               a_subln_g, b_w_in, b_w_out, ln_gain, ln_bias, router_group, router_expert,
               expert_w1, expert_w3, expert_w2)
    n_prompt = x_prompt.shape[0] * x_prompt.shape[1]
    return (y[:n_prompt].reshape(x_prompt.shape), y[n_prompt:].reshape(x_sample.shape))
```

```python
import functools
import math

import jax
import jax.numpy as jnp
from jax import lax
from jax.experimental import pallas as pl
from jax.experimental.pallas import tpu as pltpu

F32 = jnp.float32
BF16 = jnp.bfloat16

D_MODEL = 1024
DEPTH = 4
N_MIXERS = 2

A_HEADS = 8
A_HEAD_DIM = 64
A_QK_WIDTH = A_HEADS * 2 * A_HEAD_DIM
A_V_DIM = 2 * A_HEAD_DIM

B_CONFIGS = ((128, 1), (512, 4), (2048, 16))
B_GROUPS = 3
B_HEADS = 8
B_HEAD_DIM = 64
B_GROUP_WIDTH = B_HEADS * B_HEAD_DIM
B_PROJ_WIDTH = B_GROUPS * B_GROUP_WIDTH

NUM_BUCKETS = 32
MAX_DISTANCE = 1024

N_GROUPS = 4
EXPERTS_PER_GROUP = 8
N_EXPERTS = N_GROUPS * EXPERTS_PER_GROUP
D_EXPERT = 512

ALPHA = (2.0 * DEPTH) ** 0.25
LN_EPS = 1e-5
NEG = -1e30
LOG2E = math.log2(math.e)

LANES = 128
BF16_SUBLANES = 16
MOE_ROWS = 256
BIAS_TILE = 256
BIAS_TILE_SPAN = 4
VT_ROWS = A_V_DIM + BF16_SUBLANES
VMEM_LIMIT = 56 * 1024 * 1024


def _cparams(sem):
    return pltpu.CompilerParams(dimension_semantics=sem, vmem_limit_bytes=VMEM_LIMIT)


def _rel_bucket(rel):
    nb = NUM_BUCKETS // 2
    max_exact = nb // 2
    n = jnp.abs(rel)
    side = jnp.where(rel > 0, nb, 0)
    nf = jnp.maximum(n, 1).astype(F32)
    large = max_exact + (jnp.log(nf / max_exact) / math.log(MAX_DISTANCE / max_exact)
                         * (nb - max_exact)).astype(jnp.int32)
    large = jnp.minimum(large, nb - 1)
    return side + jnp.where(n < max_exact, n, large)


def _bias_lookup(cols, rel):
    bucket = _rel_bucket(rel)[None]
    shape = (cols.shape[0],) + (1,) * rel.ndim
    out = jnp.zeros((cols.shape[0],) + rel.shape, F32)
    for b in range(NUM_BUCKETS):
        out = jnp.where(bucket == b, cols[:, b].astype(F32).reshape(shape), out)
    return out


def _proj_kernel(x_ref, w_ref, s_ref, o_ref):
    acc = jnp.dot(x_ref[...].astype(BF16), w_ref[...], preferred_element_type=F32)
    o_ref[...] = (acc * s_ref[...]).astype(o_ref.dtype)


def _project(x, w, col_scale, tm=1024, tn=512):
    m, k = x.shape
    n = w.shape[1]
    return pl.pallas_call(
        _proj_kernel,
        grid=(m // tm, n // tn),
        in_specs=[pl.BlockSpec((tm, k), lambda i, j: (i, 0)),
                  pl.BlockSpec((k, tn), lambda i, j: (0, j)),
                  pl.BlockSpec((1, tn), lambda i, j: (0, j))],
        out_specs=pl.BlockSpec((tm, tn), lambda i, j: (i, j)),
        out_shape=jax.ShapeDtypeStruct((m, n), BF16),
        compiler_params=_cparams(("parallel", "arbitrary")),
    )(x, w, col_scale)


def _proj_folded_kernel(x_ref, w_ref, s_ref, o_ref, y_ref, *, fold):
    y = jnp.dot(x_ref[...].astype(BF16), w_ref[...], preferred_element_type=F32) * s_ref[...]
    if fold == 1:
        o_ref[...] = y.astype(o_ref.dtype)
        return
    n_tiles, rows, _ = y_ref.shape
    for c in range(n_tiles):
        y_ref[c] = y[:, c * LANES:(c + 1) * LANES]
    for r in range(fold):
        for c in range(n_tiles):
            col = (r * n_tiles + c) * LANES
            o_ref[:, col:col + LANES] = y_ref[c, pl.ds(r, rows // fold, stride=fold), :].astype(o_ref.dtype)


def _project_folded(x, w, col_scale, fold, tm=1024):
    m, k = x.shape
    n = w.shape[1]
    assert tm % (fold * BF16_SUBLANES) == 0 and m % tm == 0
    return pl.pallas_call(
        functools.partial(_proj_folded_kernel, fold=fold),
        grid=(m // tm,),
        in_specs=[pl.BlockSpec((tm, k), lambda i: (i, 0)),
                  pl.BlockSpec((k, n), lambda i: (0, 0)),
                  pl.BlockSpec((1, n), lambda i: (0, 0))],
        out_specs=pl.BlockSpec((tm // fold, fold * n), lambda i: (i, 0)),
        out_shape=jax.ShapeDtypeStruct((m // fold, fold * n), BF16),
        scratch_shapes=[pltpu.VMEM((n // LANES, tm, LANES), F32)],
        compiler_params=_cparams(("parallel",)),
    )(x, w, col_scale)


def _diff_attn_kernel(lam_ref, subg_ref, q_ref, k_ref, vt_ref, bt_ref, *rest, qb, kb, nk, lambda_init, aliased):
    o_ref, wq_ref, s_ref, acc_ref, m_ref = rest[1:] if aliased else rest
    qi = pl.program_id(2)
    q = q_ref[...]
    lane = lax.broadcasted_iota(jnp.int32, q.shape, 1)
    zero = jnp.zeros_like(q)
    wq_ref[...] = jnp.concatenate([jnp.where(lane < A_HEAD_DIM, q, zero),
                                   jnp.where(lane >= A_HEAD_DIM, q, zero)], axis=0)
    m_ref[...] = jnp.full(m_ref.shape, NEG, F32)
    acc_ref[...] = jnp.zeros(acc_ref.shape, F32)
    tpc = kb // BIAS_TILE
    reps = 2 * qb // BIAS_TILE
    span = BIAS_TILE_SPAN

    jb0 = jnp.maximum((qi - (span - 1)) // tpc, 0)
    jb1 = jnp.minimum((qi + (span - 1)) // tpc + 1, nk)
    n_true_band = jb1 - jb0
    far_pairs = (nk - n_true_band) // 2
    n_band = nk - 2 * far_pairs
    leftover = jnp.where(jb1 < nk, nk - 1, jb0 - 1)
    off_left = jnp.concatenate([bt_ref[0][0:1, :]] * reps, axis=1)
    off_right = jnp.concatenate([bt_ref[2 * span][0:1, :]] * reps, axis=1)
    off_zero = jnp.zeros_like(off_left)

    def band_chunk(v):
        return jnp.where(v < n_true_band, jb0 + v, leftover)

    def far_chunk(w):
        return jnp.where(w < jb0, w, w - jb0 + jb1)

    def far_off(w):
        return jnp.where(w < jb0, off_left, off_right)

    def logits(kc):
        k0 = pl.multiple_of(kc * kb, kb)
        return lax.dot_general(k_ref[pl.ds(k0, kb), :], wq_ref[...], (((1,), (1,)), ((), ())),
                               preferred_element_type=F32)

    def scores_band(kc, slot):
        s = logits(kc)
        mloc = None
        for a in range(tpc):
            e = kc * tpc + a - qi
            bt = bt_ref[jnp.clip(e, -span, span) + span]
            rows = slice(a * BIAS_TILE, (a + 1) * BIAS_TILE)
            sa = s[rows, :] + jnp.concatenate([bt] * reps, axis=1)
            s_ref[slot, rows, :] = sa
            ma = jnp.max(sa, axis=0, keepdims=True)
            mloc = ma if mloc is None else jnp.maximum(mloc, ma)
        return mloc

    def scores_far(kc, slot, off):
        s = logits(kc)
        s_ref[slot] = s
        return jnp.max(s, axis=0, keepdims=True) + off

    def update(kc, slot, mloc, off):
        m_old = m_ref[...]
        m_new = jnp.maximum(m_old, mloc)
        alpha = jnp.exp2(m_old - m_new)
        p = jnp.exp2(s_ref[slot] - (m_new - off))
        pv = jnp.dot(vt_ref[kc], p.astype(BF16), preferred_element_type=F32)
        acc_ref[...] = acc_ref[...] * alpha + pv
        m_ref[...] = m_new

    def band_pair(j, carry):
        mloc_p, kc_p = carry
        k1, k2 = band_chunk(2 * j + 1), band_chunk(2 * j + 2)
        m1 = scores_band(k1, 1)
        update(kc_p, 0, mloc_p, off_zero)
        m2 = scores_band(k2, 0)
        update(k1, 1, m1, off_zero)
        return m2, k2

    def far_pair(j, carry):
        mloc_p, kc_p, off_p = carry
        k1, k2 = far_chunk(2 * j), far_chunk(2 * j + 1)
        o1, o2 = far_off(2 * j), far_off(2 * j + 1)
        m1 = scores_far(k1, 1, o1)
        update(kc_p, 0, mloc_p, off_p)
        m2 = scores_far(k2, 0, o2)
        update(k1, 1, m1, o1)
        return m2, k2, o2

    kc0 = band_chunk(0)
    mloc_p, kc_p = lax.fori_loop(0, n_band // 2 - 1, band_pair, (scores_band(kc0, 0), kc0))
    mloc_p, kc_p, off_p = lax.fori_loop(0, far_pairs, far_pair, (mloc_p, kc_p, off_zero))
    kc_t = band_chunk(n_band - 1)
    mloc_t = scores_band(kc_t, 1)
    update(kc_p, 0, mloc_p, off_p)
    update(kc_t, 1, mloc_t, off_zero)

    lamv = lam_ref[...]
    lam = (jnp.exp(jnp.sum(lamv[0:1] * lamv[1:2], axis=1, keepdims=True))
           - jnp.exp(jnp.sum(lamv[2:3] * lamv[3:4], axis=1, keepdims=True)) + lambda_init)
    acc = acc_ref[...]
    on = acc[:A_V_DIM] / acc[A_V_DIM:A_V_DIM + 1]
    o = on[:, :qb] - lam * on[:, qb:]
    o = o * lax.rsqrt(jnp.mean(o * o, axis=0, keepdims=True) + LN_EPS)
    o = o * (subg_ref[...] * (1.0 - lambda_init))
    o_ref[...] = o.T.astype(o_ref.dtype)


def _diff_attention(qkv, row0, b, s, prev_out, bias_tiles, lam_vecs, subg_col, lambda_init, qb=256, kb=1024):
    t = qkv.shape[0]
    kb = min(kb, s // 2)
    nk = s // kb
    assert qb == BIAS_TILE and kb % BIAS_TILE == 0 and nk % 2 == 0 and nk * kb == s
    assert row0 % s == 0 and s % qb == 0
    v = qkv[row0:row0 + b * s, 2 * A_QK_WIDTH:].reshape(b, nk, kb, A_HEADS, A_V_DIM).transpose(0, 3, 1, 4, 2)
    ones = jnp.ones((b, A_HEADS, nk, 1, kb), BF16)
    pad = jnp.zeros((b, A_HEADS, nk, VT_ROWS - A_V_DIM - 1, kb), BF16)
    vt = jnp.concatenate([v, ones, pad], axis=3)
    aliased = prev_out is not None
    kern = functools.partial(_diff_attn_kernel, qb=qb, kb=kb, nk=nk, lambda_init=lambda_init, aliased=aliased)
    n_tiles = 2 * BIAS_TILE_SPAN + 1
    q_blk0, k_blk0 = row0 // qb, row0 // s
    q_blocks = s // qb
    in_specs = [pl.BlockSpec((4, A_HEAD_DIM), lambda bi, h, qi: (0, 0)),
                pl.BlockSpec((A_V_DIM, 1), lambda bi, h, qi: (0, 0)),
                pl.BlockSpec((qb, LANES), lambda bi, h, qi: (q_blk0 + bi * q_blocks + qi, h)),
                pl.BlockSpec((s, LANES), lambda bi, h, qi: (k_blk0 + bi, A_HEADS + h)),
                pl.BlockSpec((None, None, nk, VT_ROWS, kb), lambda bi, h, qi: (bi, h, 0, 0, 0)),
                pl.BlockSpec((None, n_tiles, BIAS_TILE, BIAS_TILE), lambda bi, h, qi: (h, 0, 0, 0))]
    args = [lam_vecs, subg_col, qkv, qkv, vt, bias_tiles]
    if aliased:
        in_specs.append(pl.BlockSpec(memory_space=pl.ANY))
        args.append(prev_out)
    return pl.pallas_call(
        kern,
        grid=(b, A_HEADS, q_blocks),
        in_specs=in_specs,
        out_specs=pl.BlockSpec((qb, LANES), lambda bi, h, qi: (q_blk0 + bi * q_blocks + qi, h)),
        out_shape=jax.ShapeDtypeStruct((t, A_HEADS * A_V_DIM), BF16),
        scratch_shapes=[pltpu.VMEM((2 * qb, LANES), BF16),
                        pltpu.VMEM((2, kb, 2 * qb), F32),
                        pltpu.VMEM((VT_ROWS, 2 * qb), F32),
                        pltpu.VMEM((1, 2 * qb), F32)],
        input_output_aliases={len(args) - 1: 0} if aliased else {},
        compiler_params=_cparams(("parallel", "parallel", "arbitrary")),
    )(*args)


def _diff_bias_tiles(a_cols):
    t = BIAS_TILE
    e = (jnp.arange(2 * BIAS_TILE_SPAN + 1, dtype=jnp.int32) - BIAS_TILE_SPAN) * t
    j = jnp.arange(t, dtype=jnp.int32)
    rel = e[:, None, None] + j[None, :, None] - j[None, None, :]
    return _bias_lookup(a_cols, rel) * LOG2E


def _dilated_kernel(q_ref, kp_ref, kc_ref, kn_ref, vp_ref, vc_ref, vn_ref, bias_ref, *rest,
                    lb, radius, n_blk, aliased):
    o_ref, lse_ref = rest[2:] if aliased else rest
    li = pl.program_id(2)
    win = lb + 2 * radius
    kwin = jnp.concatenate([kp_ref[lb - radius:, :], kc_ref[...], kn_ref[:radius, :]], axis=0)
    vwin = jnp.concatenate([vp_ref[lb - radius:, :], vc_ref[...], vn_ref[:radius, :]], axis=0)
    jj = lax.broadcasted_iota(jnp.int32, (lb, win), 1)
    valid = jnp.logical_and(jnp.logical_or(li > 0, jj >= radius),
                            jnp.logical_or(li < n_blk - 1, jj < lb + radius))
    q = q_ref[...]
    for h in range(B_HEADS):
        cs = slice(h * B_HEAD_DIM, (h + 1) * B_HEAD_DIM)
        s = lax.dot_general(q[:, cs], kwin[:, cs], (((1,), (1,)), ((), ())),
                            preferred_element_type=F32)
        s = jnp.where(valid, s + bias_ref[h], NEG)
        m = jnp.max(s, axis=1, keepdims=True)
        p = jnp.exp2(s - m)
        l = jnp.sum(p, axis=1, keepdims=True)
        o = jnp.dot(p.astype(BF16), vwin[:, cs], preferred_element_type=F32) / l
        o_ref[:, cs] = o
        lse_ref[:, cs] = jnp.broadcast_to(m + jnp.log2(l), (lb, B_HEAD_DIM))


def _dilated_group(x, row0, b, s, prev, dilation, radius, cols):
    t = x.shape[0] * dilation
    n_col = 3
    assert x.shape[1] == dilation * n_col * B_GROUP_WIDTH
    sub_len = s // dilation
    lb = min(256, sub_len)
    n_blk = sub_len // lb
    win = lb + 2 * radius
    assert row0 % (dilation * lb) == 0 and n_blk * lb == sub_len
    blk0 = row0 // dilation // lb
    i = jnp.arange(lb, dtype=jnp.int32)[:, None]
    j = jnp.arange(win, dtype=jnp.int32)[None, :]
    rel_sub = j - radius - i
    bias = _bias_lookup(cols, rel_sub * dilation) * LOG2E
    bias = jnp.where((jnp.abs(rel_sub) <= radius)[None], bias, NEG)

    def spec(col, shift):
        def index(bi, r, li):
            return (blk0 + bi * n_blk + jnp.clip(li + shift, 0, n_blk - 1), r * n_col + col)
        return pl.BlockSpec((lb, B_GROUP_WIDTH), index)

    aliased = prev is not None
    kern = functools.partial(_dilated_kernel, lb=lb, radius=radius, n_blk=n_blk, aliased=aliased)
    out_spec = pl.BlockSpec((lb, B_GROUP_WIDTH), lambda bi, r, li: (blk0 + bi * n_blk + li, r))
    shape = jax.ShapeDtypeStruct((t // dilation, dilation * B_GROUP_WIDTH), F32)
    in_specs = [spec(0, 0),
                spec(1, -1), spec(1, 0), spec(1, 1),
                spec(2, -1), spec(2, 0), spec(2, 1),
                pl.BlockSpec((B_HEADS, lb, win), lambda bi, r, li: (0, 0, 0))]
    args = [x, x, x, x, x, x, x, bias]
    aliases = {}
    if aliased:
        in_specs += [pl.BlockSpec(memory_space=pl.ANY)] * 2
        args += list(prev)
        aliases = {len(args) - 2: 0, len(args) - 1: 1}
    return pl.pallas_call(
        kern,
        grid=(b, dilation, n_blk),
        in_specs=in_specs,
        out_specs=[out_spec, out_spec],
        out_shape=[shape, shape],
        input_output_aliases=aliases,
        compiler_params=_cparams(("parallel", "parallel", "arbitrary")),
    )(*args)


def _layer_norm(z, g, b):
    mu = jnp.mean(z, axis=-1, keepdims=True)
    zc = z - mu
    var = jnp.mean(zc * zc, axis=-1, keepdims=True)
    return zc * lax.rsqrt(var + LN_EPS) * g + b


def _route(logits):
    lane = lax.broadcasted_iota(jnp.int32, logits.shape, 1).astype(F32)
    far = float(LANES)
    is_group = lane < N_GROUPS
    gl = jnp.where(is_group, logits, NEG)
    gmax = jnp.max(gl, axis=1, keepdims=True)
    gidx = jnp.min(jnp.where(gl == gmax, lane, far), axis=1, keepdims=True)
    gden = jnp.sum(jnp.where(is_group, jnp.exp(gl - gmax), 0.0), axis=1, keepdims=True)
    g_w = 1.0 / gden
    lo = N_GROUPS + EXPERTS_PER_GROUP * gidx
    in_group = jnp.logical_and(lane >= lo, lane < lo + EXPERTS_PER_GROUP)
    el = jnp.where(in_group, logits, NEG)
    v1 = jnp.max(el, axis=1, keepdims=True)
    i1 = jnp.min(jnp.where(el == v1, lane, far), axis=1, keepdims=True)
    el2 = jnp.where(lane == i1, NEG, el)
    v2 = jnp.max(el2, axis=1, keepdims=True)
    i2 = jnp.min(jnp.where(el2 == v2, lane, far), axis=1, keepdims=True)
    e2 = jnp.exp(v2 - v1)
    den = 1.0 + e2
    gate0 = g_w / den
    gate1 = g_w * e2 / den
    out = jnp.where(lane == 0.0, gate0, 0.0)
    out = jnp.where(lane == 1.0, gate1, out)
    out = jnp.where(lane == 2.0, i1 - N_GROUPS, out)
    out = jnp.where(lane == 3.0, i2 - N_GROUPS, out)
    return out


def _post_mixer(x_ref, y, w_ref, g_ref, b_ref, wr_ref, x1_ref, x1b_ref, route_ref):
    h = jnp.dot(y, w_ref[...], preferred_element_type=F32)
    x1 = _layer_norm(ALPHA * x_ref[...] + h, g_ref[...], b_ref[...])
    x1b = x1.astype(BF16)
    x1_ref[...] = x1
    x1b_ref[...] = x1b
    route_ref[...] = _route(jnp.dot(x1b, wr_ref[...], preferred_element_type=F32))


def _post_diff_kernel(x_ref, o_ref, w_ref, g_ref, b_ref, wr_ref, x1_ref, x1b_ref, route_ref):
    _post_mixer(x_ref, o_ref[...], w_ref, g_ref, b_ref, wr_ref, x1_ref, x1b_ref, route_ref)


def _unfold(src_ref, scratch_ref, fold):
    if fold == 1:
        return src_ref[...]
    n_tiles, rows, _ = scratch_ref.shape
    for r in range(fold):
        for c in range(n_tiles):
            col = (r * n_tiles + c) * LANES
            scratch_ref[c, pl.ds(r, rows // fold, stride=fold), :] = src_ref[:, col:col + LANES]
    return jnp.concatenate([scratch_ref[c] for c in range(n_tiles)], axis=1)


def _post_dilated_kernel(x_ref, o0_ref, o1_ref, o2_ref, l0_ref, l1_ref, l2_ref, w_ref, g_ref, b_ref, wr_ref,
                         x1_ref, x1b_ref, route_ref, *scratch):
    folds = [dil for (_, dil) in B_CONFIGS]
    o_refs, l_refs = (o0_ref, o1_ref, o2_ref), (l0_ref, l1_ref, l2_ref)
    o0, o1, o2 = [_unfold(o_refs[g], scratch[g], folds[g]) for g in range(B_GROUPS)]
    l0, l1, l2 = [_unfold(l_refs[g], scratch[B_GROUPS + g], folds[g]) for g in range(B_GROUPS)]
    m = jnp.maximum(jnp.maximum(l0, l1), l2)
    e0, e1, e2 = jnp.exp2(l0 - m), jnp.exp2(l1 - m), jnp.exp2(l2 - m)
    den = e0 + e1 + e2
    o = (o0 * (e0 / den) + o1 * (e1 / den)) + o2 * (e2 / den)
    _post_mixer(x_ref, o.astype(BF16), w_ref, g_ref, b_ref, wr_ref, x1_ref, x1b_ref, route_ref)


def _post_mixer_call(kern, x, mixer_outs, w_out, gain, bias, w_route, scratch_widths=(), tm=512):
    t, d = x.shape
    row = lambda width: pl.BlockSpec((tm, width), lambda i: (i, 0))
    folded = lambda a: pl.BlockSpec((tm * a.shape[0] // t, a.shape[1]), lambda i: (i, 0))
    whole = lambda a: pl.BlockSpec(a.shape, lambda i: (0,) * a.ndim)
    return pl.pallas_call(
        kern,
        grid=(t // tm,),
        in_specs=[row(d)] + [folded(a) for a in mixer_outs]
        + [whole(w_out), whole(gain), whole(bias), whole(w_route)],
        out_specs=[row(d), row(d), row(LANES)],
        out_shape=[jax.ShapeDtypeStruct((t, d), F32), jax.ShapeDtypeStruct((t, d), BF16),
                   jax.ShapeDtypeStruct((t, LANES), F32)],
        scratch_shapes=[pltpu.VMEM((width // LANES, tm, LANES), F32) for width in scratch_widths],
        compiler_params=_cparams(("parallel",)),
    )(x, *mixer_outs, w_out, gain, bias, w_route)


def _moe_mlp_kernel(blk_ref, used_ref, x_ref, gw_ref, w1_ref, w3_ref, w2_ref, y_ref, w1b_ref, w3b_ref, w2b_ref):
    i = pl.program_id(0)
    active = i < used_ref[0]
    new_expert = jnp.logical_or(i == 0, blk_ref[i] != blk_ref[jnp.maximum(i - 1, 0)])

    @pl.when(jnp.logical_and(active, new_expert))
    def _():
        w1b_ref[...] = w1_ref[...].astype(BF16)
        w3b_ref[...] = w3_ref[...].astype(BF16)
        w2b_ref[...] = w2_ref[...].astype(BF16)

    @pl.when(active)
    def _():
        x = x_ref[...]
        h1 = jnp.dot(x, w1b_ref[...], preferred_element_type=F32)
        h3 = jnp.dot(x, w3b_ref[...], preferred_element_type=F32)
        h = (h1 * jax.nn.sigmoid(h1)) * h3
        y = jnp.dot(h.astype(BF16), w2b_ref[...], preferred_element_type=F32)
        y_ref[...] = (y * gw_ref[...]).astype(y_ref.dtype)

    @pl.when(jnp.logical_not(active))
    def _():
        y_ref[...] = jnp.zeros(y_ref.shape, y_ref.dtype)


def _moe_mlp(xs, gw, blk_exp, n_used, layer, w1, w3, w2):
    p, d = xs.shape
    n_blk = p // MOE_ROWS
    grid_spec = pltpu.PrefetchScalarGridSpec(
        num_scalar_prefetch=2,
        grid=(n_blk,),
        in_specs=[pl.BlockSpec((MOE_ROWS, d), lambda i, be, nu: (i, 0)),
                  pl.BlockSpec((MOE_ROWS, 1), lambda i, be, nu: (i, 0)),
                  pl.BlockSpec((None, None, d, D_EXPERT), lambda i, be, nu: (layer, be[i], 0, 0)),
                  pl.BlockSpec((None, None, d, D_EXPERT), lambda i, be, nu: (layer, be[i], 0, 0)),
                  pl.BlockSpec((None, None, D_EXPERT, d), lambda i, be, nu: (layer, be[i], 0, 0))],
        out_specs=pl.BlockSpec((MOE_ROWS, d), lambda i, be, nu: (i, 0)),
        scratch_shapes=[pltpu.VMEM((d, D_EXPERT), BF16), pltpu.VMEM((d, D_EXPERT), BF16),
                        pltpu.VMEM((D_EXPERT, d), BF16)],
    )
    return pl.pallas_call(
        _moe_mlp_kernel,
        grid_spec=grid_spec,
        out_shape=jax.ShapeDtypeStruct((p, d), BF16),
        compiler_params=_cparams(("arbitrary",)),
    )(blk_exp, n_used, xs, gw, w1, w3, w2)


def _combine_ln_kernel(x_ref, ya_ref, yb_ref, g_ref, b_ref, o_ref):
    z = ALPHA * x_ref[...] + (ya_ref[...].astype(F32) + yb_ref[...].astype(F32))
    o_ref[...] = _layer_norm(z, g_ref[...], b_ref[...])


def _combine_ln(x1, ya, yb, gain, bias, tm=1024):
    t, d = x1.shape
    row = pl.BlockSpec((tm, d), lambda i: (i, 0))
    vec = pl.BlockSpec((1, d), lambda i: (0, 0))
    return pl.pallas_call(
        _combine_ln_kernel,
        grid=(t // tm,),
        in_specs=[row, row, row, vec, vec],
        out_specs=row,
        out_shape=jax.ShapeDtypeStruct((t, d), F32),
        compiler_params=_cparams(("parallel",)),
    )(x1, ya, yb, gain, bias)


def _moe(x1, x1b, route, layer, w1, w3, w2, gain, bias):
    t = x1.shape[0]
    n_assign = 2 * t
    take = lambda a, idx: a.at[idx].get(mode='promise_in_bounds')
    gate = route[:, 0:2].reshape(n_assign)
    e = route[:, 2:4].astype(jnp.int32).reshape(n_assign)
    order = jnp.argsort(e).astype(jnp.int32)
    rank = jnp.argsort(order).astype(jnp.int32)
    experts = jnp.arange(N_EXPERTS, dtype=jnp.int32)
    counts = jnp.sum((e[:, None] == experts[None, :]).astype(jnp.int32), axis=0)
    padded = (counts + MOE_ROWS - 1) // MOE_ROWS * MOE_ROWS
    starts = jnp.cumsum(counts) - counts
    pends = jnp.cumsum(padded)
    pstarts = pends - padded
    pos = (take(pstarts - starts, e) + rank).reshape(t, 2)
    n_rows = n_assign + N_EXPERTS * MOE_ROWS
    n_blk = n_rows // MOE_ROWS
    blk_row0 = jnp.arange(n_blk, dtype=jnp.int32) * MOE_ROWS
    blk_exp = jnp.minimum(jnp.sum((pends[None, :] <= blk_row0[:, None]).astype(jnp.int32), axis=1), N_EXPERTS - 1)
    n_used = (pends[N_EXPERTS - 1] // MOE_ROWS).astype(jnp.int32)[None]
    base = jnp.repeat(take(starts - pstarts, blk_exp), MOE_ROWS)
    limit = jnp.repeat(take(starts + counts, blk_exp), MOE_ROWS)
    sidx = jnp.arange(n_rows, dtype=jnp.int32) + base
    valid = sidx < limit
    a_src = take(order, jnp.clip(sidx, 0, n_assign - 1))
    src_tok = jnp.where(valid, a_src // 2, 0)
    gw = jnp.where(valid, take(gate, a_src), 0.0)
    xs = take(x1b, src_tok)
    y = _moe_mlp(xs, gw[:, None], blk_exp, n_used, layer, w1, w3, w2)
    ya = take(y, pos[:, 0])
    yb = take(y, pos[:, 1])
    return _combine_ln(x1, ya, yb, gain, bias)


def _col_scale(width, n_scaled, scale):
    return jnp.where(jnp.arange(width) < n_scaled, scale, 1.0).astype(F32)[None, :]


def _trunk(x, batch_shapes, rel_bias_table, a_w_in, a_w_out, a_lambda_q1, a_lambda_k1, a_lambda_q2, a_lambda_k2,
           a_subln_g, b_w_in, b_w_out, ln_gain, ln_bias, router_group, router_expert,
           expert_w1, expert_w3, expert_w2):
    a_cols = rel_bias_table[:, :A_HEADS].T
    bias_tiles = _diff_bias_tiles(a_cols)
    a_scale = _col_scale(3 * A_QK_WIDTH, A_QK_WIDTH, A_HEAD_DIM ** -0.5 * LOG2E)
    b_scale = _col_scale(3 * B_GROUP_WIDTH, B_GROUP_WIDTH, B_HEAD_DIM ** -0.5 * LOG2E)
    offsets = []
    start = 0
    for (b, s) in batch_shapes:
        offsets.append((start, b, s))
        start += b * s
    t = x.shape[0]

    for i in range(DEPTH):
        li = i // N_MIXERS
        w_route = jnp.concatenate(
            [router_group[i], router_expert[i],
             jnp.zeros((D_MODEL, LANES - N_GROUPS - N_EXPERTS), F32)], axis=1).astype(BF16)
        gain1, bias1 = ln_gain[i, 0][None, :], ln_bias[i, 0][None, :]
        gain2, bias2 = ln_gain[i, 1][None, :], ln_bias[i, 1][None, :]
        if i % N_MIXERS == 0:
            lambda_init = 0.8 - 0.6 * math.exp(-0.3 * i)
            qkv = _project(x, a_w_in[li].astype(BF16), a_scale)
            lam_vecs = jnp.stack([a_lambda_q1[li], a_lambda_k1[li], a_lambda_q2[li], a_lambda_k2[li]]).astype(F32)
            subg_col = a_subln_g[li].astype(F32)[:, None]
            o = None
            for (st, b, s) in offsets:
                o = _diff_attention(qkv, st, b, s, o, bias_tiles, lam_vecs, subg_col, lambda_init)
            mixer_outs = [o]
            kern = _post_diff_kernel
            w_out = a_w_out[li].astype(BF16)
            scratch_widths = ()
        else:
            outs, lses = [], []
            for g, (window, dil) in enumerate(B_CONFIGS):
                c0 = A_HEADS + g * B_HEADS
                cols = rel_bias_table[:, c0:c0 + B_HEADS].T
                w_g = jnp.concatenate([b_w_in[li][:, (sec * B_GROUPS + g) * B_GROUP_WIDTH:
                                                  (sec * B_GROUPS + g + 1) * B_GROUP_WIDTH] for sec in range(3)],
                                      axis=1).astype(BF16)
                qkv_g = _project_folded(x, w_g, b_scale, dil)
                pair = None
                for (st, b, s) in offsets:
                    pair = _dilated_group(qkv_g, st, b, s, pair, dil, window // (2 * dil), cols)
                outs.append(pair[0])
                lses.append(pair[1])
            mixer_outs = outs + lses
            kern = _post_dilated_kernel
            w_out = b_w_out[li].astype(BF16)
            scratch_widths = (B_GROUP_WIDTH,) * (2 * B_GROUPS)
        x1, x1b, route = _post_mixer_call(kern, x, mixer_outs, w_out, gain1, bias1, w_route, scratch_widths)
        x = _moe(x1, x1b, route, i, expert_w1, expert_w3, expert_w2, gain2, bias2)
    return x


def kernel(x_prompt, x_sample, rel_bias_table, a_w_in, a_w_out, a_lambda_q1, a_lambda_k1, a_lambda_q2, a_lambda_k2, a_subln_g, b_w_in, b_w_out, ln_gain, ln_bias, router_group, router_expert, expert_w1, expert_w3, expert_w2):
    shapes = [x_prompt.shape[:2], x_sample.shape[:2]]
    x = jnp.concatenate([x_prompt.reshape(-1, D_MODEL), x_sample.reshape(-1, D_MODEL)], axis=0)
    y = _trunk(x, shapes, rel_bias_table, a_w_in, a_w_out, a_lambda_q1, a_lambda_k1, a_lambda_q2, a_lambda_k2,
               a_subln_g, b_w_in, b_w_out, ln_gain, ln_bias, router_group, router_expert,
               expert_w1, expert_w3, expert_w2)
    n_prompt = x_prompt.shape[0] * x_prompt.shape[1]
    return (y[:n_prompt].reshape(x_prompt.shape), y[n_prompt:].reshape(x_sample.shape))
```

```python
import functools
import math

import jax
import jax.numpy as jnp
from jax import lax
from jax.experimental import pallas as pl
from jax.experimental.pallas import tpu as pltpu

F32 = jnp.float32
BF16 = jnp.bfloat16

D_MODEL = 1024
DEPTH = 4
N_MIXERS = 2

A_HEADS = 8
A_HEAD_DIM = 64
A_QK_WIDTH = A_HEADS * 2 * A_HEAD_DIM
A_V_DIM = 2 * A_HEAD_DIM

B_CONFIGS = ((128, 1), (512, 4), (2048, 16))
B_GROUPS = 3
B_HEADS = 8
B_HEAD_DIM = 64
B_GROUP_WIDTH = B_HEADS * B_HEAD_DIM
B_PROJ_WIDTH = B_GROUPS * B_GROUP_WIDTH

NUM_BUCKETS = 32
MAX_DISTANCE = 1024

N_GROUPS = 4
EXPERTS_PER_GROUP = 8
N_EXPERTS = N_GROUPS * EXPERTS_PER_GROUP
D_EXPERT = 512

ALPHA = (2.0 * DEPTH) ** 0.25
LN_EPS = 1e-5
NEG = -1e30
LOG2E = math.log2(math.e)

LANES = 128
BF16_SUBLANES = 16
MOE_ROWS = 256
BIAS_TILE = 256
BIAS_TILE_SPAN = 4
VT_ROWS = A_V_DIM + BF16_SUBLANES
VMEM_LIMIT = 56 * 1024 * 1024


def _cparams(sem):
    return pltpu.CompilerParams(dimension_semantics=sem, vmem_limit_bytes=VMEM_LIMIT)


def _rel_bucket(rel):
    nb = NUM_BUCKETS // 2
    max_exact = nb // 2
    n = jnp.abs(rel)
    side = jnp.where(rel > 0, nb, 0)
    nf = jnp.maximum(n, 1).astype(F32)
    large = max_exact + (jnp.log(nf / max_exact) / math.log(MAX_DISTANCE / max_exact)
                         * (nb - max_exact)).astype(jnp.int32)
    large = jnp.minimum(large, nb - 1)
    return side + jnp.where(n < max_exact, n, large)


def _bias_lookup(cols, rel):
    bucket = _rel_bucket(rel)[None]
    shape = (cols.shape[0],) + (1,) * rel.ndim
    out = jnp.zeros((cols.shape[0],) + rel.shape, F32)
    for b in range(NUM_BUCKETS):
        out = jnp.where(bucket == b, cols[:, b].astype(F32).reshape(shape), out)
    return out


def _proj_kernel(x_ref, w_ref, s_ref, o_ref):
    acc = jnp.dot(x_ref[...].astype(BF16), w_ref[...], preferred_element_type=F32)
    o_ref[...] = (acc * s_ref[...]).astype(o_ref.dtype)


def _project(x, w, col_scale, tm=1024, tn=512):
    m, k = x.shape
    n = w.shape[1]
    return pl.pallas_call(
        _proj_kernel,
        grid=(m // tm, n // tn),
        in_specs=[pl.BlockSpec((tm, k), lambda i, j: (i, 0)),
                  pl.BlockSpec((k, tn), lambda i, j: (0, j)),
                  pl.BlockSpec((1, tn), lambda i, j: (0, j))],
        out_specs=pl.BlockSpec((tm, tn), lambda i, j: (i, j)),
        out_shape=jax.ShapeDtypeStruct((m, n), BF16),
        compiler_params=_cparams(("parallel", "arbitrary")),
    )(x, w, col_scale)


def _proj_folded_kernel(x_ref, w_ref, s_ref, o_ref, y_ref, *, fold):
    y = jnp.dot(x_ref[...].astype(BF16), w_ref[...], preferred_element_type=F32) * s_ref[...]
    if fold == 1:
        o_ref[...] = y.astype(o_ref.dtype)
        return
    n_tiles, rows, _ = y_ref.shape
    for c in range(n_tiles):
        y_ref[c] = y[:, c * LANES:(c + 1) * LANES]
    for r in range(fold):
        for c in range(n_tiles):
            col = (r * n_tiles + c) * LANES
            o_ref[:, col:col + LANES] = y_ref[c, pl.ds(r, rows // fold, stride=fold), :].astype(o_ref.dtype)


def _project_folded(x, w, col_scale, fold, tm=1024):
    m, k = x.shape
    n = w.shape[1]
    assert tm % (fold * BF16_SUBLANES) == 0 and m % tm == 0
    return pl.pallas_call(
        functools.partial(_proj_folded_kernel, fold=fold),
        grid=(m // tm,),
        in_specs=[pl.BlockSpec((tm, k), lambda i: (i, 0)),
                  pl.BlockSpec((k, n), lambda i: (0, 0)),
                  pl.BlockSpec((1, n), lambda i: (0, 0))],
        out_specs=pl.BlockSpec((tm // fold, fold * n), lambda i: (i, 0)),
        out_shape=jax.ShapeDtypeStruct((m // fold, fold * n), BF16),
        scratch_shapes=[pltpu.VMEM((n // LANES, tm, LANES), F32)],
        compiler_params=_cparams(("parallel",)),
    )(x, w, col_scale)


def _diff_attn_kernel(lam_ref, subg_ref, q_ref, k_ref, vt_ref, bt_ref, *rest, qb, kb, nk, lambda_init, aliased):
    o_ref, wq_ref, s_ref, acc_ref, m_ref = rest[1:] if aliased else rest
    qi = pl.program_id(2)
    q = q_ref[...]
    lane = lax.broadcasted_iota(jnp.int32, q.shape, 1)
    zero = jnp.zeros_like(q)
    wq_ref[...] = jnp.concatenate([jnp.where(lane < A_HEAD_DIM, q, zero),
                                   jnp.where(lane >= A_HEAD_DIM, q, zero)], axis=0)
    m_ref[...] = jnp.full(m_ref.shape, NEG, F32)
    acc_ref[...] = jnp.zeros(acc_ref.shape, F32)
    tpc = kb // BIAS_TILE
    reps = 2 * qb // BIAS_TILE
    span = BIAS_TILE_SPAN

    jb0 = jnp.maximum((qi - (span - 1)) // tpc, 0)
    jb1 = jnp.minimum((qi + (span - 1)) // tpc + 1, nk)
    n_true_band = jb1 - jb0
    far_pairs = (nk - n_true_band) // 2
    n_band = nk - 2 * far_pairs
    leftover = jnp.where(jb1 < nk, nk - 1, jb0 - 1)
    off_left = jnp.concatenate([bt_ref[0][0:1, :]] * reps, axis=1)
    off_right = jnp.concatenate([bt_ref[2 * span][0:1, :]] * reps, axis=1)
    off_zero = jnp.zeros_like(off_left)

    def band_chunk(v):
        return jnp.where(v < n_true_band, jb0 + v, leftover)

    def far_chunk(w):
        return jnp.where(w < jb0, w, w - jb0 + jb1)

    def far_off(w):
        return jnp.where(w < jb0, off_left, off_right)

    def logits(kc):
        k0 = pl.multiple_of(kc * kb, kb)
        return lax.dot_general(k_ref[pl.ds(k0, kb), :], wq_ref[...], (((1,), (1,)), ((), ())),
                               preferred_element_type=F32)

    def scores_band(kc, slot):
        s = logits(kc)
        mloc = None
        for a in range(tpc):
            e = kc * tpc + a - qi
            bt = bt_ref[jnp.clip(e, -span, span) + span]
            rows = slice(a * BIAS_TILE, (a + 1) * BIAS_TILE)
            sa = s[rows, :] + jnp.concatenate([bt] * reps, axis=1)
            s_ref[slot, rows, :] = sa
            ma = jnp.max(sa, axis=0, keepdims=True)
            mloc = ma if mloc is None else jnp.maximum(mloc, ma)
        return mloc

    def scores_far(kc, slot, off):
        s = logits(kc)
        s_ref[slot] = s
        return jnp.max(s, axis=0, keepdims=True) + off

    def update(kc, slot, mloc, off):
        m_old = m_ref[...]
        m_new = jnp.maximum(m_old, mloc)
        alpha = jnp.exp2(m_old - m_new)
        p = jnp.exp2(s_ref[slot] - (m_new - off))
        pv = jnp.dot(vt_ref[kc], p.astype(BF16), preferred_element_type=F32)
        acc_ref[...] = acc_ref[...] * alpha + pv
        m_ref[...] = m_new

    def band_pair(j, carry):
        mloc_p, kc_p = carry
        k1, k2 = band_chunk(2 * j + 1), band_chunk(2 * j + 2)
        m1 = scores_band(k1, 1)
        update(kc_p, 0, mloc_p, off_zero)
        m2 = scores_band(k2, 0)
        update(k1, 1, m1, off_zero)
        return m2, k2

    def far_pair(j, carry):
        mloc_p, kc_p, off_p = carry
        k1, k2 = far_chunk(2 * j), far_chunk(2 * j + 1)
        o1, o2 = far_off(2 * j), far_off(2 * j + 1)
        m1 = scores_far(k1, 1, o1)
        update(kc_p, 0, mloc_p, off_p)
        m2 = scores_far(k2, 0, o2)
        update(k1, 1, m1, o1)
        return m2, k2, o2

    kc0 = band_chunk(0)
    mloc_p, kc_p = lax.fori_loop(0, n_band // 2 - 1, band_pair, (scores_band(kc0, 0), kc0))
    mloc_p, kc_p, off_p = lax.fori_loop(0, far_pairs, far_pair, (mloc_p, kc_p, off_zero))
    kc_t = band_chunk(n_band - 1)
    mloc_t = scores_band(kc_t, 1)
    update(kc_p, 0, mloc_p, off_p)
    update(kc_t, 1, mloc_t, off_zero)

    lamv = lam_ref[...]
    lam = (jnp.exp(jnp.sum(lamv[0:1] * lamv[1:2], axis=1, keepdims=True))
           - jnp.exp(jnp.sum(lamv[2:3] * lamv[3:4], axis=1, keepdims=True)) + lambda_init)
    acc = acc_ref[...]
    on = acc[:A_V_DIM] / acc[A_V_DIM:A_V_DIM + 1]
    o = on[:, :qb] - lam * on[:, qb:]
    o = o * lax.rsqrt(jnp.mean(o * o, axis=0, keepdims=True) + LN_EPS)
    o = o * (subg_ref[...] * (1.0 - lambda_init))
    o_ref[...] = o.T.astype(o_ref.dtype)


def _diff_attention(qkv, row0, b, s, prev_out, bias_tiles, lam_vecs, subg_col, lambda_init, qb=256, kb=1024):
    t = qkv.shape[0]
    kb = min(kb, s // 2)
    nk = s // kb
    assert qb == BIAS_TILE and kb % BIAS_TILE == 0 and nk % 2 == 0 and nk * kb == s
    assert row0 % s == 0 and s % qb == 0
    v = qkv[row0:row0 + b * s, 2 * A_QK_WIDTH:].reshape(b, nk, kb, A_HEADS, A_V_DIM).transpose(0, 3, 1, 4, 2)
    ones = jnp.ones((b, A_HEADS, nk, 1, kb), BF16)
    pad = jnp.zeros((b, A_HEADS, nk, VT_ROWS - A_V_DIM - 1, kb), BF16)
    vt = jnp.concatenate([v, ones, pad], axis=3)
    aliased = prev_out is not None
    kern = functools.partial(_diff_attn_kernel, qb=qb, kb=kb, nk=nk, lambda_init=lambda_init, aliased=aliased)
    n_tiles = 2 * BIAS_TILE_SPAN + 1
    q_blk0, k_blk0 = row0 // qb, row0 // s
    q_blocks = s // qb
    in_specs = [pl.BlockSpec((4, A_HEAD_DIM), lambda bi, h, qi: (0, 0)),
                pl.BlockSpec((A_V_DIM, 1), lambda bi, h, qi: (0, 0)),
                pl.BlockSpec((qb, LANES), lambda bi, h, qi: (q_blk0 + bi * q_blocks + qi, h)),
                pl.BlockSpec((s, LANES), lambda bi, h, qi: (k_blk0 + bi, A_HEADS + h)),
                pl.BlockSpec((None, None, nk, VT_ROWS, kb), lambda bi, h, qi: (bi, h, 0, 0, 0)),
                pl.BlockSpec((None, n_tiles, BIAS_TILE, BIAS_TILE), lambda bi, h, qi: (h, 0, 0, 0))]
    args = [lam_vecs, subg_col, qkv, qkv, vt, bias_tiles]
    if aliased:
        in_specs.append(pl.BlockSpec(memory_space=pl.ANY))
        args.append(prev_out)
    return pl.pallas_call(
        kern,
        grid=(b, A_HEADS, q_blocks),
        in_specs=in_specs,
        out_specs=pl.BlockSpec((qb, LANES), lambda bi, h, qi: (q_blk0 + bi * q_blocks + qi, h)),
        out_shape=jax.ShapeDtypeStruct((t, A_HEADS * A_V_DIM), BF16),
        scratch_shapes=[pltpu.VMEM((2 * qb, LANES), BF16),
                        pltpu.VMEM((2, kb, 2 * qb), F32),
                        pltpu.VMEM((VT_ROWS, 2 * qb), F32),
                        pltpu.VMEM((1, 2 * qb), F32)],
        input_output_aliases={len(args) - 1: 0} if aliased else {},
        compiler_params=_cparams(("parallel", "parallel", "arbitrary")),
    )(*args)


def _diff_bias_tiles(a_cols):
    t = BIAS_TILE
    e = (jnp.arange(2 * BIAS_TILE_SPAN + 1, dtype=jnp.int32) - BIAS_TILE_SPAN) * t
    j = jnp.arange(t, dtype=jnp.int32)
    rel = e[:, None, None] + j[None, :, None] - j[None, None, :]
    return _bias_lookup(a_cols, rel) * LOG2E


def _dilated_kernel(q_ref, kp_ref, kc_ref, kn_ref, vp_ref, vc_ref, vn_ref, bias_ref, *rest,
                    lb, sb, radius, n_blk, aliased):
    o_ref, lse_ref = rest[2:] if aliased else rest
    li = pl.program_id(2)
    kwin = jnp.concatenate([kp_ref[lb - radius:, :], kc_ref[...], kn_ref[:radius, :]], axis=0)
    vwin = jnp.concatenate([vp_ref[lb - radius:, :], vc_ref[...], vn_ref[:radius, :]], axis=0)
    q = q_ref[...]
    swin = sb + 2 * radius
    jj = lax.broadcasted_iota(jnp.int32, (swin, sb), 0)
    for u in range(lb // sb):
        rows = slice(u * sb, u * sb + swin)
        valid = jnp.logical_and(jnp.logical_or(li > 0, jj + u * sb >= radius),
                                jnp.logical_or(li < n_blk - 1, jj + u * sb < lb + radius))
        outs, lses = [], []
        for h in range(B_HEADS):
            cs = slice(h * B_HEAD_DIM, (h + 1) * B_HEAD_DIM)
            s = lax.dot_general(kwin[rows, cs], q[u * sb:(u + 1) * sb, cs], (((1,), (1,)), ((), ())),
                                preferred_element_type=F32)
            s = jnp.where(valid, s + bias_ref[h], NEG)
            m = jnp.max(s, axis=0, keepdims=True)
            p = jnp.exp2(s - m)
            l = jnp.sum(p, axis=0, keepdims=True)
            o = lax.dot_general(vwin[rows, cs], p.astype(BF16), (((0,), (0,)), ((), ())),
                                preferred_element_type=F32)
            outs.append(o / l)
            lses.append(jnp.broadcast_to(m + jnp.log2(l), (B_HEAD_DIM, sb)))
        o_ref[u * sb:(u + 1) * sb, :] = jnp.concatenate(outs, axis=0).T
        lse_ref[u * sb:(u + 1) * sb, :] = jnp.concatenate(lses, axis=0).T


def _dilated_group(x, row0, b, s, prev, dilation, radius, cols):
    t = x.shape[0] * dilation
    n_col = 3
    assert x.shape[1] == dilation * n_col * B_GROUP_WIDTH
    sub_len = s // dilation
    lb = min(512, sub_len)
    sb = lb
    n_blk = sub_len // lb
    win = sb + 2 * radius
    assert row0 % (dilation * lb) == 0 and n_blk * lb == sub_len and lb % sb == 0
    blk0 = row0 // dilation // lb
    i = jnp.arange(sb, dtype=jnp.int32)[None, :]
    j = jnp.arange(win, dtype=jnp.int32)[:, None]
    rel_sub = j - radius - i
    bias = _bias_lookup(cols, rel_sub * dilation) * LOG2E
    bias = jnp.where((jnp.abs(rel_sub) <= radius)[None], bias, NEG)

    def spec(col, shift):
        def index(bi, r, li):
            return (blk0 + bi * n_blk + jnp.clip(li + shift, 0, n_blk - 1), r * n_col + col)
        return pl.BlockSpec((lb, B_GROUP_WIDTH), index)

    aliased = prev is not None
    kern = functools.partial(_dilated_kernel, lb=lb, sb=sb, radius=radius, n_blk=n_blk, aliased=aliased)
    out_spec = pl.BlockSpec((lb, B_GROUP_WIDTH), lambda bi, r, li: (blk0 + bi * n_blk + li, r))
    shape = jax.ShapeDtypeStruct((t // dilation, dilation * B_GROUP_WIDTH), F32)
    in_specs = [spec(0, 0),
                spec(1, -1), spec(1, 0), spec(1, 1),
                spec(2, -1), spec(2, 0), spec(2, 1),
                pl.BlockSpec((B_HEADS, win, sb), lambda bi, r, li: (0, 0, 0))]
    args = [x, x, x, x, x, x, x, bias]
    aliases = {}
    if aliased:
        in_specs += [pl.BlockSpec(memory_space=pl.ANY)] * 2
        args += list(prev)
        aliases = {len(args) - 2: 0, len(args) - 1: 1}
    return pl.pallas_call(
        kern,
        grid=(b, dilation, n_blk),
        in_specs=in_specs,
        out_specs=[out_spec, out_spec],
        out_shape=[shape, shape],
        input_output_aliases=aliases,
        compiler_params=_cparams(("parallel", "parallel", "arbitrary")),
    )(*args)


def _layer_norm(z, g, b):
    mu = jnp.mean(z, axis=-1, keepdims=True)
    zc = z - mu
    var = jnp.mean(zc * zc, axis=-1, keepdims=True)
    return zc * lax.rsqrt(var + LN_EPS) * g + b


def _route(logits):
    lane = lax.broadcasted_iota(jnp.int32, logits.shape, 1).astype(F32)
    far = float(LANES)
    is_group = lane < N_GROUPS
    gl = jnp.where(is_group, logits, NEG)
    gmax = jnp.max(gl, axis=1, keepdims=True)
    gidx = jnp.min(jnp.where(gl == gmax, lane, far), axis=1, keepdims=True)
    gden = jnp.sum(jnp.where(is_group, jnp.exp(gl - gmax), 0.0), axis=1, keepdims=True)
    g_w = 1.0 / gden
    lo = N_GROUPS + EXPERTS_PER_GROUP * gidx
    in_group = jnp.logical_and(lane >= lo, lane < lo + EXPERTS_PER_GROUP)
    el = jnp.where(in_group, logits, NEG)
    v1 = jnp.max(el, axis=1, keepdims=True)
    i1 = jnp.min(jnp.where(el == v1, lane, far), axis=1, keepdims=True)
    el2 = jnp.where(lane == i1, NEG, el)
    v2 = jnp.max(el2, axis=1, keepdims=True)
    i2 = jnp.min(jnp.where(el2 == v2, lane, far), axis=1, keepdims=True)
    e2 = jnp.exp(v2 - v1)
    den = 1.0 + e2
    gate0 = g_w / den
    gate1 = g_w * e2 / den
    out = jnp.where(lane == 0.0, gate0, 0.0)
    out = jnp.where(lane == 1.0, gate1, out)
    out = jnp.where(lane == 2.0, i1 - N_GROUPS, out)
    out = jnp.where(lane == 3.0, i2 - N_GROUPS, out)
    return out


def _post_mixer(x_ref, y, w_ref, g_ref, b_ref, wr_ref, x1_ref, x1b_ref, route_ref):
    h = jnp.dot(y, w_ref[...], preferred_element_type=F32)
    x1 = _layer_norm(ALPHA * x_ref[...] + h, g_ref[...], b_ref[...])
    x1b = x1.astype(BF16)
    x1_ref[...] = x1
    x1b_ref[...] = x1b
    route_ref[...] = _route(jnp.dot(x1b, wr_ref[...], preferred_element_type=F32))


def _post_diff_kernel(x_ref, o_ref, w_ref, g_ref, b_ref, wr_ref, x1_ref, x1b_ref, route_ref):
    _post_mixer(x_ref, o_ref[...], w_ref, g_ref, b_ref, wr_ref, x1_ref, x1b_ref, route_ref)


def _unfold(src_ref, scratch_ref, fold):
    if fold == 1:
        return src_ref[...]
    n_tiles, rows, _ = scratch_ref.shape
    for r in range(fold):
        for c in range(n_tiles):
            col = (r * n_tiles + c) * LANES
            scratch_ref[c, pl.ds(r, rows // fold, stride=fold), :] = src_ref[:, col:col + LANES]
    return jnp.concatenate([scratch_ref[c] for c in range(n_tiles)], axis=1)


def _post_dilated_kernel(x_ref, o0_ref, o1_ref, o2_ref, l0_ref, l1_ref, l2_ref, w_ref, g_ref, b_ref, wr_ref,
                         x1_ref, x1b_ref, route_ref, *scratch):
    folds = [dil for (_, dil) in B_CONFIGS]
    o_refs, l_refs = (o0_ref, o1_ref, o2_ref), (l0_ref, l1_ref, l2_ref)
    o0, o1, o2 = [_unfold(o_refs[g], scratch[g], folds[g]) for g in range(B_GROUPS)]
    l0, l1, l2 = [_unfold(l_refs[g], scratch[B_GROUPS + g], folds[g]) for g in range(B_GROUPS)]
    m = jnp.maximum(jnp.maximum(l0, l1), l2)
    e0, e1, e2 = jnp.exp2(l0 - m), jnp.exp2(l1 - m), jnp.exp2(l2 - m)
    den = e0 + e1 + e2
    o = (o0 * (e0 / den) + o1 * (e1 / den)) + o2 * (e2 / den)
    _post_mixer(x_ref, o.astype(BF16), w_ref, g_ref, b_ref, wr_ref, x1_ref, x1b_ref, route_ref)


def _post_mixer_call(kern, x, mixer_outs, w_out, gain, bias, w_route, scratch_widths=(), tm=512):
    t, d = x.shape
    row = lambda width: pl.BlockSpec((tm, width), lambda i: (i, 0))
    folded = lambda a: pl.BlockSpec((tm * a.shape[0] // t, a.shape[1]), lambda i: (i, 0))
    whole = lambda a: pl.BlockSpec(a.shape, lambda i: (0,) * a.ndim)
    return pl.pallas_call(
        kern,
        grid=(t // tm,),
        in_specs=[row(d)] + [folded(a) for a in mixer_outs]
        + [whole(w_out), whole(gain), whole(bias), whole(w_route)],
        out_specs=[row(d), row(d), row(LANES)],
        out_shape=[jax.ShapeDtypeStruct((t, d), F32), jax.ShapeDtypeStruct((t, d), BF16),
                   jax.ShapeDtypeStruct((t, LANES), F32)],
        scratch_shapes=[pltpu.VMEM((width // LANES, tm, LANES), F32) for width in scratch_widths],
        compiler_params=_cparams(("parallel",)),
    )(x, *mixer_outs, w_out, gain, bias, w_route)


def _moe_mlp_kernel(blk_ref, used_ref, x_ref, gw_ref, w1_ref, w3_ref, w2_ref, y_ref, w1b_ref, w3b_ref, w2b_ref):
    i = pl.program_id(0)
    active = i < used_ref[0]
    new_expert = jnp.logical_or(i == 0, blk_ref[i] != blk_ref[jnp.maximum(i - 1, 0)])

    @pl.when(jnp.logical_and(active, new_expert))
    def _():
        w1b_ref[...] = w1_ref[...].astype(BF16)
        w3b_ref[...] = w3_ref[...].astype(BF16)
        w2b_ref[...] = w2_ref[...].astype(BF16)

    @pl.when(active)
    def _():
        x = x_ref[...]
        h1 = jnp.dot(x, w1b_ref[...], preferred_element_type=F32)
        h3 = jnp.dot(x, w3b_ref[...], preferred_element_type=F32)
        h = (h1 * jax.nn.sigmoid(h1)) * h3
        y = jnp.dot(h.astype(BF16), w2b_ref[...], preferred_element_type=F32)
        y_ref[...] = (y * gw_ref[...]).astype(y_ref.dtype)

    @pl.when(jnp.logical_not(active))
    def _():
        y_ref[...] = jnp.zeros(y_ref.shape, y_ref.dtype)


def _moe_mlp(xs, gw, blk_exp, n_used, layer, w1, w3, w2):
    p, d = xs.shape
    n_blk = p // MOE_ROWS
    grid_spec = pltpu.PrefetchScalarGridSpec(
        num_scalar_prefetch=2,
        grid=(n_blk,),
        in_specs=[pl.BlockSpec((MOE_ROWS, d), lambda i, be, nu: (i, 0)),
                  pl.BlockSpec((MOE_ROWS, 1), lambda i, be, nu: (i, 0)),
                  pl.BlockSpec((None, None, d, D_EXPERT), lambda i, be, nu: (layer, be[i], 0, 0)),
                  pl.BlockSpec((None, None, d, D_EXPERT), lambda i, be, nu: (layer, be[i], 0, 0)),
                  pl.BlockSpec((None, None, D_EXPERT, d), lambda i, be, nu: (layer, be[i], 0, 0))],
        out_specs=pl.BlockSpec((MOE_ROWS, d), lambda i, be, nu: (i, 0)),
        scratch_shapes=[pltpu.VMEM((d, D_EXPERT), BF16), pltpu.VMEM((d, D_EXPERT), BF16),
                        pltpu.VMEM((D_EXPERT, d), BF16)],
    )
    return pl.pallas_call(
        _moe_mlp_kernel,
        grid_spec=grid_spec,
        out_shape=jax.ShapeDtypeStruct((p, d), BF16),
        compiler_params=_cparams(("arbitrary",)),
    )(blk_exp, n_used, xs, gw, w1, w3, w2)


def _combine_ln_kernel(x_ref, ya_ref, yb_ref, g_ref, b_ref, *rest):
    o_ref = rest[-1]
    z = ALPHA * x_ref[...] + (ya_ref[...].astype(F32) + yb_ref[...].astype(F32))
    o_ref[...] = _layer_norm(z, g_ref[...], b_ref[...])


def _combine_ln(x1, row0, ya, yb, gain, bias, prev_out, tm=1024):
    t, d = x1.shape
    n = ya.shape[0]
    assert row0 % tm == 0 and n % tm == 0
    blk0 = row0 // tm
    row_x = pl.BlockSpec((tm, d), lambda i: (blk0 + i, 0))
    row_y = pl.BlockSpec((tm, d), lambda i: (i, 0))
    vec = pl.BlockSpec((1, d), lambda i: (0, 0))
    in_specs = [row_x, row_y, row_y, vec, vec]
    args = [x1, ya, yb, gain, bias]
    aliases = {}
    if prev_out is not None:
        in_specs.append(pl.BlockSpec(memory_space=pl.ANY))
        args.append(prev_out)
        aliases = {len(args) - 1: 0}
    return pl.pallas_call(
        _combine_ln_kernel,
        grid=(n // tm,),
        in_specs=in_specs,
        out_specs=row_x,
        out_shape=jax.ShapeDtypeStruct((t, d), F32),
        input_output_aliases=aliases,
        compiler_params=_cparams(("parallel",)),
    )(*args)


def _moe(x1, x1b, route, row0, n_tok, prev_out, layer, w1, w3, w2, gain, bias):
    t = n_tok
    n_assign = 2 * t
    take = lambda a, idx: a.at[idx].get(mode='promise_in_bounds')
    route = lax.slice_in_dim(route, row0, row0 + t, axis=0)
    gate = route[:, 0:2].reshape(n_assign)
    e = route[:, 2:4].astype(jnp.int32).reshape(n_assign)
    order = jnp.argsort(e).astype(jnp.int32)
    rank = jnp.argsort(order).astype(jnp.int32)
    experts = jnp.arange(N_EXPERTS, dtype=jnp.int32)
    counts = jnp.sum((e[:, None] == experts[None, :]).astype(jnp.int32), axis=0)
    padded = (counts + MOE_ROWS - 1) // MOE_ROWS * MOE_ROWS
    starts = jnp.cumsum(counts) - counts
    pends = jnp.cumsum(padded)
    pstarts = pends - padded
    pos = (take(pstarts - starts, e) + rank).reshape(t, 2)
    n_rows = n_assign + N_EXPERTS * MOE_ROWS
    n_blk = n_rows // MOE_ROWS
    blk_row0 = jnp.arange(n_blk, dtype=jnp.int32) * MOE_ROWS
    blk_exp = jnp.minimum(jnp.sum((pends[None, :] <= blk_row0[:, None]).astype(jnp.int32), axis=1), N_EXPERTS - 1)
    n_used = (pends[N_EXPERTS - 1] // MOE_ROWS).astype(jnp.int32)[None]
    base = jnp.repeat(take(starts - pstarts, blk_exp), MOE_ROWS)
    limit = jnp.repeat(take(starts + counts, blk_exp), MOE_ROWS)
    sidx = jnp.arange(n_rows, dtype=jnp.int32) + base
    valid = sidx < limit
    a_src = take(order, jnp.clip(sidx, 0, n_assign - 1))
    src_tok = jnp.where(valid, a_src // 2, 0) + row0
    gw = jnp.where(valid, take(gate, a_src), 0.0)
    xs = take(x1b, src_tok)
    y = _moe_mlp(xs, gw[:, None], blk_exp, n_used, layer, w1, w3, w2)
    ya = take(y, pos[:, 0])
    yb = take(y, pos[:, 1])
    return _combine_ln(x1, row0, ya, yb, gain, bias, prev_out)


def _col_scale(width, n_scaled, scale):
    return jnp.where(jnp.arange(width) < n_scaled, scale, 1.0).astype(F32)[None, :]


def _trunk(x, batch_shapes, rel_bias_table, a_w_in, a_w_out, a_lambda_q1, a_lambda_k1, a_lambda_q2, a_lambda_k2,
           a_subln_g, b_w_in, b_w_out, ln_gain, ln_bias, router_group, router_expert,
           expert_w1, expert_w3, expert_w2):
    a_cols = rel_bias_table[:, :A_HEADS].T
    bias_tiles = _diff_bias_tiles(a_cols)
    a_scale = _col_scale(3 * A_QK_WIDTH, A_QK_WIDTH, A_HEAD_DIM ** -0.5 * LOG2E)
    b_scale = _col_scale(3 * B_GROUP_WIDTH, B_GROUP_WIDTH, B_HEAD_DIM ** -0.5 * LOG2E)
    offsets = []
    start = 0
    for (b, s) in batch_shapes:
        offsets.append((start, b, s))
        start += b * s
    t = x.shape[0]

    for i in range(DEPTH):
        li = i // N_MIXERS
        w_route = jnp.concatenate(
            [router_group[i], router_expert[i],
             jnp.zeros((D_MODEL, LANES - N_GROUPS - N_EXPERTS), F32)], axis=1).astype(BF16)
        gain1, bias1 = ln_gain[i, 0][None, :], ln_bias[i, 0][None, :]
        gain2, bias2 = ln_gain[i, 1][None, :], ln_bias[i, 1][None, :]
        if i % N_MIXERS == 0:
            lambda_init = 0.8 - 0.6 * math.exp(-0.3 * i)
            qkv = _project(x, a_w_in[li].astype(BF16), a_scale)
            lam_vecs = jnp.stack([a_lambda_q1[li], a_lambda_k1[li], a_lambda_q2[li], a_lambda_k2[li]]).astype(F32)
            subg_col = a_subln_g[li].astype(F32)[:, None]
            o = None
            for (st, b, s) in offsets:
                o = _diff_attention(qkv, st, b, s, o, bias_tiles, lam_vecs, subg_col, lambda_init)
            mixer_outs = [o]
            kern = _post_diff_kernel
            w_out = a_w_out[li].astype(BF16)
            scratch_widths = ()
        else:
            outs, lses = [], []
            for g, (window, dil) in enumerate(B_CONFIGS):
                c0 = A_HEADS + g * B_HEADS
                cols = rel_bias_table[:, c0:c0 + B_HEADS].T
                w_g = jnp.concatenate([b_w_in[li][:, (sec * B_GROUPS + g) * B_GROUP_WIDTH:
                                                  (sec * B_GROUPS + g + 1) * B_GROUP_WIDTH] for sec in range(3)],
                                      axis=1).astype(BF16)
                qkv_g = _project_folded(x, w_g, b_scale, dil)
                pair = None
                for (st, b, s) in offsets:
                    pair = _dilated_group(qkv_g, st, b, s, pair, dil, window // (2 * dil), cols)
                outs.append(pair[0])
                lses.append(pair[1])
            mixer_outs = outs + lses
            kern = _post_dilated_kernel
            w_out = b_w_out[li].astype(BF16)
            scratch_widths = (B_GROUP_WIDTH,) * (2 * B_GROUPS)
        x1, x1b, route = _post_mixer_call(kern, x, mixer_outs, w_out, gain1, bias1, w_route, scratch_widths)
        x = None
        for (st, b, s) in offsets:
            x = _moe(x1, x1b, route, st, b * s, x, i, expert_w1, expert_w3, expert_w2, gain2, bias2)
    return x


def kernel(x_prompt, x_sample, rel_bias_table, a_w_in, a_w_out, a_lambda_q1, a_lambda_k1, a_lambda_q2, a_lambda_k2, a_subln_g, b_w_in, b_w_out, ln_gain, ln_bias, router_group, router_expert, expert_w1, expert_w3, expert_w2):
    shapes = [x_prompt.shape[:2], x_sample.shape[:2]]
    x = jnp.concatenate([x_prompt.reshape(-1, D_MODEL), x_sample.reshape(-1, D_MODEL)], axis=0)
    y = _trunk(x, shapes, rel_bias_table, a_w_in, a_w_out, a_lambda_q1, a_lambda_k1, a_lambda_q2, a_lambda_k2,
               a_subln_g, b_w_in, b_w_out, ln_gain, ln_bias, router_group, router_expert,
               expert_w1, expert_w3, expert_w2)
    n_prompt = x_prompt.shape[0] * x_prompt.shape[1]
    return (y[:n_prompt].reshape(x_prompt.shape), y[n_prompt:].reshape(x_sample.shape))
```

```python
import functools
import math

import jax
import jax.numpy as jnp
from jax import lax
from jax.experimental import pallas as pl
from jax.experimental.pallas import tpu as pltpu

F32 = jnp.float32
BF16 = jnp.bfloat16

D_MODEL = 1024
DEPTH = 4
N_MIXERS = 2

A_HEADS = 8
A_HEAD_DIM = 64
A_QK_WIDTH = A_HEADS * 2 * A_HEAD_DIM
A_V_DIM = 2 * A_HEAD_DIM

B_CONFIGS = ((128, 1), (512, 4), (2048, 16))
B_GROUPS = 3
B_HEADS = 8
B_HEAD_DIM = 64
B_GROUP_WIDTH = B_HEADS * B_HEAD_DIM
B_PROJ_WIDTH = B_GROUPS * B_GROUP_WIDTH

NUM_BUCKETS = 32
MAX_DISTANCE = 1024

N_GROUPS = 4
EXPERTS_PER_GROUP = 8
N_EXPERTS = N_GROUPS * EXPERTS_PER_GROUP
D_EXPERT = 512

ALPHA = (2.0 * DEPTH) ** 0.25
LN_EPS = 1e-5
NEG = -1e30
LOG2E = math.log2(math.e)

LANES = 128
BF16_SUBLANES = 16
MOE_ROWS = 512
BIAS_TILE = 256
BIAS_TILE_SPAN = 4
VT_ROWS = A_V_DIM + BF16_SUBLANES
VMEM_LIMIT = 56 * 1024 * 1024


def _cparams(sem):
    return pltpu.CompilerParams(dimension_semantics=sem, vmem_limit_bytes=VMEM_LIMIT)


def _rel_bucket(rel):
    nb = NUM_BUCKETS // 2
    max_exact = nb // 2
    n = jnp.abs(rel)
    side = jnp.where(rel > 0, nb, 0)
    nf = jnp.maximum(n, 1).astype(F32)
    large = max_exact + (jnp.log(nf / max_exact) / math.log(MAX_DISTANCE / max_exact)
                         * (nb - max_exact)).astype(jnp.int32)
    large = jnp.minimum(large, nb - 1)
    return side + jnp.where(n < max_exact, n, large)


def _bias_lookup(cols, rel):
    bucket = _rel_bucket(rel)[None]
    shape = (cols.shape[0],) + (1,) * rel.ndim
    out = jnp.zeros((cols.shape[0],) + rel.shape, F32)
    for b in range(NUM_BUCKETS):
        out = jnp.where(bucket == b, cols[:, b].astype(F32).reshape(shape), out)
    return out


def _proj_kernel(x_ref, w_ref, s_ref, o_ref):
    acc = jnp.dot(x_ref[...].astype(BF16), w_ref[...], preferred_element_type=F32)
    o_ref[...] = (acc * s_ref[...]).astype(o_ref.dtype)


def _project(x, w, col_scale, tm=1024, tn=512):
    m, k = x.shape
    n = w.shape[1]
    return pl.pallas_call(
        _proj_kernel,
        grid=(m // tm, n // tn),
        in_specs=[pl.BlockSpec((tm, k), lambda i, j: (i, 0)),
                  pl.BlockSpec((k, tn), lambda i, j: (0, j)),
                  pl.BlockSpec((1, tn), lambda i, j: (0, j))],
        out_specs=pl.BlockSpec((tm, tn), lambda i, j: (i, j)),
        out_shape=jax.ShapeDtypeStruct((m, n), BF16),
        compiler_params=_cparams(("parallel", "arbitrary")),
    )(x, w, col_scale)


def _proj_folded_kernel(x_ref, w_ref, s_ref, o_ref, y_ref, *, fold):
    y = jnp.dot(x_ref[...].astype(BF16), w_ref[...], preferred_element_type=F32) * s_ref[...]
    if fold == 1:
        o_ref[...] = y.astype(o_ref.dtype)
        return
    n_tiles, rows, _ = y_ref.shape
    for c in range(n_tiles):
        y_ref[c] = y[:, c * LANES:(c + 1) * LANES]
    for r in range(fold):
        for c in range(n_tiles):
            col = (r * n_tiles + c) * LANES
            o_ref[:, col:col + LANES] = y_ref[c, pl.ds(r, rows // fold, stride=fold), :].astype(o_ref.dtype)


def _project_folded(x, w, col_scale, fold, tm=1024):
    m, k = x.shape
    n = w.shape[1]
    assert tm % (fold * BF16_SUBLANES) == 0 and m % tm == 0
    return pl.pallas_call(
        functools.partial(_proj_folded_kernel, fold=fold),
        grid=(m // tm,),
        in_specs=[pl.BlockSpec((tm, k), lambda i: (i, 0)),
                  pl.BlockSpec((k, n), lambda i: (0, 0)),
                  pl.BlockSpec((1, n), lambda i: (0, 0))],
        out_specs=pl.BlockSpec((tm // fold, fold * n), lambda i: (i, 0)),
        out_shape=jax.ShapeDtypeStruct((m // fold, fold * n), BF16),
        scratch_shapes=[pltpu.VMEM((n // LANES, tm, LANES), F32)],
        compiler_params=_cparams(("parallel",)),
    )(x, w, col_scale)


def _diff_attn_kernel(lam_ref, subg_ref, q_ref, k_ref, vt_ref, bt_ref, *rest, qb, kb, nk, lambda_init, aliased):
    o_ref, wq_ref, s_ref, acc_ref, m_ref = rest[1:] if aliased else rest
    qi = pl.program_id(2)
    q = q_ref[...]
    lane = lax.broadcasted_iota(jnp.int32, q.shape, 1)
    zero = jnp.zeros_like(q)
    wq_ref[...] = jnp.concatenate([jnp.where(lane < A_HEAD_DIM, q, zero),
                                   jnp.where(lane >= A_HEAD_DIM, q, zero)], axis=0)
    m_ref[...] = jnp.full(m_ref.shape, NEG, F32)
    acc_ref[...] = jnp.zeros(acc_ref.shape, F32)
    tpc = kb // BIAS_TILE
    qt = qb // BIAS_TILE
    reps = 2 * qt
    span = BIAS_TILE_SPAN
    q_tile0 = qi * qt

    jb0 = jnp.maximum((q_tile0 - (span - 1)) // tpc, 0)
    jb1 = jnp.minimum((q_tile0 + qt - 1 + span - 1) // tpc + 1, nk)
    n_true_band = jb1 - jb0
    far_pairs = (nk - n_true_band) // 2
    n_band = nk - 2 * far_pairs
    leftover = jnp.where(jb1 < nk, nk - 1, jb0 - 1)
    off_left = jnp.concatenate([bt_ref[0][0:1, :]] * reps, axis=1)
    off_right = jnp.concatenate([bt_ref[2 * span][0:1, :]] * reps, axis=1)
    off_zero = jnp.zeros_like(off_left)

    def band_chunk(v):
        return jnp.where(v < n_true_band, jb0 + v, leftover)

    def far_chunk(w):
        return jnp.where(w < jb0, w, w - jb0 + jb1)

    def far_off(w):
        return jnp.where(w < jb0, off_left, off_right)

    def logits(kc):
        k0 = pl.multiple_of(kc * kb, kb)
        return lax.dot_general(k_ref[pl.ds(k0, kb), :], wq_ref[...], (((1,), (1,)), ((), ())),
                               preferred_element_type=F32)

    def scores_band(kc, slot):
        s = logits(kc)
        mloc = None
        for a in range(tpc):
            tiles = [bt_ref[jnp.clip(kc * tpc + a - (q_tile0 + b), -span, span) + span] for b in range(qt)]
            rows = slice(a * BIAS_TILE, (a + 1) * BIAS_TILE)
            sa = s[rows, :] + jnp.concatenate(tiles * 2, axis=1)
            s_ref[slot, rows, :] = sa
            ma = jnp.max(sa, axis=0, keepdims=True)
            mloc = ma if mloc is None else jnp.maximum(mloc, ma)
        return mloc

    def scores_far(kc, slot, off):
        s = logits(kc)
        s_ref[slot] = s
        return jnp.max(s, axis=0, keepdims=True) + off

    def update(kc, slot, mloc, off):
        m_old = m_ref[...]
        m_new = jnp.maximum(m_old, mloc)
        alpha = jnp.exp2(m_old - m_new)
        p = jnp.exp2(s_ref[slot] - (m_new - off))
        pv = jnp.dot(vt_ref[kc], p.astype(BF16), preferred_element_type=F32)
        acc_ref[...] = acc_ref[...] * alpha + pv
        m_ref[...] = m_new

    def band_pair(j, carry):
        mloc_p, kc_p = carry
        k1, k2 = band_chunk(2 * j + 1), band_chunk(2 * j + 2)
        m1 = scores_band(k1, 1)
        update(kc_p, 0, mloc_p, off_zero)
        m2 = scores_band(k2, 0)
        update(k1, 1, m1, off_zero)
        return m2, k2

    def far_pair(j, carry):
        mloc_p, kc_p, off_p = carry
        k1, k2 = far_chunk(2 * j), far_chunk(2 * j + 1)
        o1, o2 = far_off(2 * j), far_off(2 * j + 1)
        m1 = scores_far(k1, 1, o1)
        update(kc_p, 0, mloc_p, off_p)
        m2 = scores_far(k2, 0, o2)
        update(k1, 1, m1, o1)
        return m2, k2, o2

    kc0 = band_chunk(0)
    mloc_p, kc_p = lax.fori_loop(0, n_band // 2 - 1, band_pair, (scores_band(kc0, 0), kc0))
    mloc_p, kc_p, off_p = lax.fori_loop(0, far_pairs, far_pair, (mloc_p, kc_p, off_zero))
    kc_t = band_chunk(n_band - 1)
    mloc_t = scores_band(kc_t, 1)
    update(kc_p, 0, mloc_p, off_p)
    update(kc_t, 1, mloc_t, off_zero)

    lamv = lam_ref[...]
    lam = (jnp.exp(jnp.sum(lamv[0:1] * lamv[1:2], axis=1, keepdims=True))
           - jnp.exp(jnp.sum(lamv[2:3] * lamv[3:4], axis=1, keepdims=True)) + lambda_init)
    acc = acc_ref[...]
    on = acc[:A_V_DIM] / acc[A_V_DIM:A_V_DIM + 1]
    o = on[:, :qb] - lam * on[:, qb:]
    o = o * lax.rsqrt(jnp.mean(o * o, axis=0, keepdims=True) + LN_EPS)
    o = o * (subg_ref[...] * (1.0 - lambda_init))
    o_ref[...] = o.T.astype(o_ref.dtype)


def _diff_attention(qkv, row0, b, s, prev_out, bias_tiles, lam_vecs, subg_col, lambda_init, qb=512, kb=512):
    t = qkv.shape[0]
    kb = min(kb, s // 2)
    nk = s // kb
    assert qb % BIAS_TILE == 0 and kb % BIAS_TILE == 0 and nk % 2 == 0 and nk * kb == s
    assert row0 % s == 0 and s % qb == 0
    v = qkv[row0:row0 + b * s, 2 * A_QK_WIDTH:].reshape(b, nk, kb, A_HEADS, A_V_DIM).transpose(0, 3, 1, 4, 2)
    ones = jnp.ones((b, A_HEADS, nk, 1, kb), BF16)
    pad = jnp.zeros((b, A_HEADS, nk, VT_ROWS - A_V_DIM - 1, kb), BF16)
    vt = jnp.concatenate([v, ones, pad], axis=3)
    aliased = prev_out is not None
    kern = functools.partial(_diff_attn_kernel, qb=qb, kb=kb, nk=nk, lambda_init=lambda_init, aliased=aliased)
    n_tiles = 2 * BIAS_TILE_SPAN + 1
    q_blk0, k_blk0 = row0 // qb, row0 // s
    q_blocks = s // qb
    in_specs = [pl.BlockSpec((4, A_HEAD_DIM), lambda bi, h, qi: (0, 0)),
                pl.BlockSpec((A_V_DIM, 1), lambda bi, h, qi: (0, 0)),
                pl.BlockSpec((qb, LANES), lambda bi, h, qi: (q_blk0 + bi * q_blocks + qi, h)),
                pl.BlockSpec((s, LANES), lambda bi, h, qi: (k_blk0 + bi, A_HEADS + h)),
                pl.BlockSpec((None, None, nk, VT_ROWS, kb), lambda bi, h, qi: (bi, h, 0, 0, 0)),
                pl.BlockSpec((None, n_tiles, BIAS_TILE, BIAS_TILE), lambda bi, h, qi: (h, 0, 0, 0))]
    args = [lam_vecs, subg_col, qkv, qkv, vt, bias_tiles]
    if aliased:
        in_specs.append(pl.BlockSpec(memory_space=pl.ANY))
        args.append(prev_out)
    return pl.pallas_call(
        kern,
        grid=(b, A_HEADS, q_blocks),
        in_specs=in_specs,
        out_specs=pl.BlockSpec((qb, LANES), lambda bi, h, qi: (q_blk0 + bi * q_blocks + qi, h)),
        out_shape=jax.ShapeDtypeStruct((t, A_HEADS * A_V_DIM), BF16),
        scratch_shapes=[pltpu.VMEM((2 * qb, LANES), BF16),
                        pltpu.VMEM((2, kb, 2 * qb), F32),
                        pltpu.VMEM((VT_ROWS, 2 * qb), F32),
                        pltpu.VMEM((1, 2 * qb), F32)],
        input_output_aliases={len(args) - 1: 0} if aliased else {},
        compiler_params=_cparams(("parallel", "parallel", "arbitrary")),
    )(*args)


def _diff_bias_tiles(a_cols):
    t = BIAS_TILE
    e = (jnp.arange(2 * BIAS_TILE_SPAN + 1, dtype=jnp.int32) - BIAS_TILE_SPAN) * t
    j = jnp.arange(t, dtype=jnp.int32)
    rel = e[:, None, None] + j[None, :, None] - j[None, None, :]
    return _bias_lookup(a_cols, rel) * LOG2E


def _dilated_kernel(q_ref, kp_ref, kc_ref, kn_ref, vp_ref, vc_ref, vn_ref, bias_ref, *rest,
                    lb, sb, radius, n_blk, aliased):
    o_ref, lse_ref = rest[2:] if aliased else rest
    li = pl.program_id(2)
    kwin = jnp.concatenate([kp_ref[lb - radius:, :], kc_ref[...], kn_ref[:radius, :]], axis=0)
    vwin = jnp.concatenate([vp_ref[lb - radius:, :], vc_ref[...], vn_ref[:radius, :]], axis=0)
    q = q_ref[...]
    swin = sb + 2 * radius
    jj = lax.broadcasted_iota(jnp.int32, (swin, sb), 0)
    for u in range(lb // sb):
        rows = slice(u * sb, u * sb + swin)
        valid = jnp.logical_and(jnp.logical_or(li > 0, jj + u * sb >= radius),
                                jnp.logical_or(li < n_blk - 1, jj + u * sb < lb + radius))
        outs, lses = [], []
        for h in range(B_HEADS):
            cs = slice(h * B_HEAD_DIM, (h + 1) * B_HEAD_DIM)
            s = lax.dot_general(kwin[rows, cs], q[u * sb:(u + 1) * sb, cs], (((1,), (1,)), ((), ())),
                                preferred_element_type=F32)
            s = jnp.where(valid, s + bias_ref[h], NEG)
            m = jnp.max(s, axis=0, keepdims=True)
            p = jnp.exp2(s - m)
            l = jnp.sum(p, axis=0, keepdims=True)
            o = lax.dot_general(vwin[rows, cs], p.astype(BF16), (((0,), (0,)), ((), ())),
                                preferred_element_type=F32)
            outs.append(o / l)
            lses.append(jnp.broadcast_to(m + jnp.log2(l), (B_HEAD_DIM, sb)))
        o_ref[u * sb:(u + 1) * sb, :] = jnp.concatenate(outs, axis=0).T
        lse_ref[u * sb:(u + 1) * sb, :] = jnp.concatenate(lses, axis=0).T


def _dilated_group(x, row0, b, s, prev, dilation, radius, cols):
    t = x.shape[0] * dilation
    n_col = 3
    assert x.shape[1] == dilation * n_col * B_GROUP_WIDTH
    sub_len = s // dilation
    lb = min(512, sub_len)
    sb = lb
    n_blk = sub_len // lb
    win = sb + 2 * radius
    assert row0 % (dilation * lb) == 0 and n_blk * lb == sub_len and lb % sb == 0
    blk0 = row0 // dilation // lb
    i = jnp.arange(sb, dtype=jnp.int32)[None, :]
    j = jnp.arange(win, dtype=jnp.int32)[:, None]
    rel_sub = j - radius - i
    bias = _bias_lookup(cols, rel_sub * dilation) * LOG2E
    bias = jnp.where((jnp.abs(rel_sub) <= radius)[None], bias, NEG)

    def spec(col, shift):
        def index(bi, r, li):
            return (blk0 + bi * n_blk + jnp.clip(li + shift, 0, n_blk - 1), r * n_col + col)
        return pl.BlockSpec((lb, B_GROUP_WIDTH), index)

    aliased = prev is not None
    kern = functools.partial(_dilated_kernel, lb=lb, sb=sb, radius=radius, n_blk=n_blk, aliased=aliased)
    out_spec = pl.BlockSpec((lb, B_GROUP_WIDTH), lambda bi, r, li: (blk0 + bi * n_blk + li, r))
    shape = jax.ShapeDtypeStruct((t // dilation, dilation * B_GROUP_WIDTH), F32)
    in_specs = [spec(0, 0),
                spec(1, -1), spec(1, 0), spec(1, 1),
                spec(2, -1), spec(2, 0), spec(2, 1),
                pl.BlockSpec((B_HEADS, win, sb), lambda bi, r, li: (0, 0, 0))]
    args = [x, x, x, x, x, x, x, bias]
    aliases = {}
    if aliased:
        in_specs += [pl.BlockSpec(memory_space=pl.ANY)] * 2
        args += list(prev)
        aliases = {len(args) - 2: 0, len(args) - 1: 1}
    return pl.pallas_call(
        kern,
        grid=(b, dilation, n_blk),
        in_specs=in_specs,
        out_specs=[out_spec, out_spec],
        out_shape=[shape, shape],
        input_output_aliases=aliases,
        compiler_params=_cparams(("parallel", "parallel", "arbitrary")),
    )(*args)


def _layer_norm(z, g, b):
    mu = jnp.mean(z, axis=-1, keepdims=True)
    zc = z - mu
    var = jnp.mean(zc * zc, axis=-1, keepdims=True)
    return zc * lax.rsqrt(var + LN_EPS) * g + b


def _route(logits):
    lane = lax.broadcasted_iota(jnp.int32, logits.shape, 1).astype(F32)
    far = float(LANES)
    is_group = lane < N_GROUPS
    gl = jnp.where(is_group, logits, NEG)
    gmax = jnp.max(gl, axis=1, keepdims=True)
    gidx = jnp.min(jnp.where(gl == gmax, lane, far), axis=1, keepdims=True)
    gden = jnp.sum(jnp.where(is_group, jnp.exp(gl - gmax), 0.0), axis=1, keepdims=True)
    g_w = 1.0 / gden
    lo = N_GROUPS + EXPERTS_PER_GROUP * gidx
    in_group = jnp.logical_and(lane >= lo, lane < lo + EXPERTS_PER_GROUP)
    el = jnp.where(in_group, logits, NEG)
    v1 = jnp.max(el, axis=1, keepdims=True)
    i1 = jnp.min(jnp.where(el == v1, lane, far), axis=1, keepdims=True)
    el2 = jnp.where(lane == i1, NEG, el)
    v2 = jnp.max(el2, axis=1, keepdims=True)
    i2 = jnp.min(jnp.where(el2 == v2, lane, far), axis=1, keepdims=True)
    e2 = jnp.exp(v2 - v1)
    den = 1.0 + e2
    gate0 = g_w / den
    gate1 = g_w * e2 / den
    out = jnp.where(lane == 0.0, gate0, 0.0)
    out = jnp.where(lane == 1.0, gate1, out)
    out = jnp.where(lane == 2.0, i1 - N_GROUPS, out)
    out = jnp.where(lane == 3.0, i2 - N_GROUPS, out)
    return out


def _post_mixer(x_ref, y, w_ref, g_ref, b_ref, wr_ref, x1_ref, x1b_ref, route_ref):
    h = jnp.dot(y, w_ref[...], preferred_element_type=F32)
    x1 = _layer_norm(ALPHA * x_ref[...] + h, g_ref[...], b_ref[...])
    x1b = x1.astype(BF16)
    x1_ref[...] = x1
    x1b_ref[...] = x1b
    route_ref[...] = _route(jnp.dot(x1b, wr_ref[...], preferred_element_type=F32))


def _post_diff_kernel(x_ref, o_ref, w_ref, g_ref, b_ref, wr_ref, x1_ref, x1b_ref, route_ref):
    _post_mixer(x_ref, o_ref[...], w_ref, g_ref, b_ref, wr_ref, x1_ref, x1b_ref, route_ref)


def _unfold(src_ref, scratch_ref, fold):
    if fold == 1:
        return src_ref[...]
    n_tiles, rows, _ = scratch_ref.shape
    for r in range(fold):
        for c in range(n_tiles):
            col = (r * n_tiles + c) * LANES
            scratch_ref[c, pl.ds(r, rows // fold, stride=fold), :] = src_ref[:, col:col + LANES]
    return jnp.concatenate([scratch_ref[c] for c in range(n_tiles)], axis=1)


def _post_dilated_kernel(x_ref, o0_ref, o1_ref, o2_ref, l0_ref, l1_ref, l2_ref, w_ref, g_ref, b_ref, wr_ref,
                         x1_ref, x1b_ref, route_ref, *scratch):
    folds = [dil for (_, dil) in B_CONFIGS]
    o_refs, l_refs = (o0_ref, o1_ref, o2_ref), (l0_ref, l1_ref, l2_ref)
    o0, o1, o2 = [_unfold(o_refs[g], scratch[g], folds[g]) for g in range(B_GROUPS)]
    l0, l1, l2 = [_unfold(l_refs[g], scratch[B_GROUPS + g], folds[g]) for g in range(B_GROUPS)]
    m = jnp.maximum(jnp.maximum(l0, l1), l2)
    e0, e1, e2 = jnp.exp2(l0 - m), jnp.exp2(l1 - m), jnp.exp2(l2 - m)
    den = e0 + e1 + e2
    o = (o0 * (e0 / den) + o1 * (e1 / den)) + o2 * (e2 / den)
    _post_mixer(x_ref, o.astype(BF16), w_ref, g_ref, b_ref, wr_ref, x1_ref, x1b_ref, route_ref)


def _post_mixer_call(kern, x, mixer_outs, w_out, gain, bias, w_route, scratch_widths=(), tm=512):
    t, d = x.shape
    row = lambda width: pl.BlockSpec((tm, width), lambda i: (i, 0))
    folded = lambda a: pl.BlockSpec((tm * a.shape[0] // t, a.shape[1]), lambda i: (i, 0))
    whole = lambda a: pl.BlockSpec(a.shape, lambda i: (0,) * a.ndim)
    return pl.pallas_call(
        kern,
        grid=(t // tm,),
        in_specs=[row(d)] + [folded(a) for a in mixer_outs]
        + [whole(w_out), whole(gain), whole(bias), whole(w_route)],
        out_specs=[row(d), row(d), row(LANES)],
        out_shape=[jax.ShapeDtypeStruct((t, d), F32), jax.ShapeDtypeStruct((t, d), BF16),
                   jax.ShapeDtypeStruct((t, LANES), F32)],
        scratch_shapes=[pltpu.VMEM((width // LANES, tm, LANES), F32) for width in scratch_widths],
        compiler_params=_cparams(("parallel",)),
    )(x, *mixer_outs, w_out, gain, bias, w_route)


def _moe_mlp_kernel(blk_ref, used_ref, x_ref, gw_ref, w1_ref, w3_ref, w2_ref, y_ref, w1b_ref, w3b_ref, w2b_ref):
    i = pl.program_id(0)
    active = i < used_ref[0]
    new_expert = jnp.logical_or(i == 0, blk_ref[i] != blk_ref[jnp.maximum(i - 1, 0)])

    @pl.when(jnp.logical_and(active, new_expert))
    def _():
        w1b_ref[...] = w1_ref[...].astype(BF16)
        w3b_ref[...] = w3_ref[...].astype(BF16)
        w2b_ref[...] = w2_ref[...].astype(BF16)

    @pl.when(active)
    def _():
        x = x_ref[...]
        h1 = jnp.dot(x, w1b_ref[...], preferred_element_type=F32)
        h3 = jnp.dot(x, w3b_ref[...], preferred_element_type=F32)
        h = (h1 * jax.nn.sigmoid(h1)) * h3
        y = jnp.dot(h.astype(BF16), w2b_ref[...], preferred_element_type=F32)
        y_ref[...] = (y * gw_ref[...]).astype(y_ref.dtype)

    @pl.when(jnp.logical_not(active))
    def _():
        y_ref[...] = jnp.zeros(y_ref.shape, y_ref.dtype)


def _moe_mlp(xs, gw, blk_exp, n_used, layer, w1, w3, w2):
    p, d = xs.shape
    n_blk = p // MOE_ROWS
    grid_spec = pltpu.PrefetchScalarGridSpec(
        num_scalar_prefetch=2,
        grid=(n_blk,),
        in_specs=[pl.BlockSpec((MOE_ROWS, d), lambda i, be, nu: (i, 0)),
                  pl.BlockSpec((MOE_ROWS, 1), lambda i, be, nu: (i, 0)),
                  pl.BlockSpec((None, None, d, D_EXPERT), lambda i, be, nu: (layer, be[i], 0, 0)),
                  pl.BlockSpec((None, None, d, D_EXPERT), lambda i, be, nu: (layer, be[i], 0, 0)),
                  pl.BlockSpec((None, None, D_EXPERT, d), lambda i, be, nu: (layer, be[i], 0, 0))],
        out_specs=pl.BlockSpec((MOE_ROWS, d), lambda i, be, nu: (i, 0)),
        scratch_shapes=[pltpu.VMEM((d, D_EXPERT), BF16), pltpu.VMEM((d, D_EXPERT), BF16),
                        pltpu.VMEM((D_EXPERT, d), BF16)],
    )
    return pl.pallas_call(
        _moe_mlp_kernel,
        grid_spec=grid_spec,
        out_shape=jax.ShapeDtypeStruct((p, d), BF16),
        compiler_params=_cparams(("arbitrary",)),
    )(blk_exp, n_used, xs, gw, w1, w3, w2)


def _combine_ln_kernel(x_ref, ya_ref, yb_ref, g_ref, b_ref, *rest):
    o_ref = rest[-1]
    z = ALPHA * x_ref[...] + (ya_ref[...].astype(F32) + yb_ref[...].astype(F32))
    o_ref[...] = _layer_norm(z, g_ref[...], b_ref[...])


def _combine_ln(x1, row0, ya, yb, gain, bias, prev_out, tm=1024):
    t, d = x1.shape
    n = ya.shape[0]
    assert row0 % tm == 0 and n % tm == 0
    blk0 = row0 // tm
    row_x = pl.BlockSpec((tm, d), lambda i: (blk0 + i, 0))
    row_y = pl.BlockSpec((tm, d), lambda i: (i, 0))
    vec = pl.BlockSpec((1, d), lambda i: (0, 0))
    in_specs = [row_x, row_y, row_y, vec, vec]
    args = [x1, ya, yb, gain, bias]
    aliases = {}
    if prev_out is not None:
        in_specs.append(pl.BlockSpec(memory_space=pl.ANY))
        args.append(prev_out)
        aliases = {len(args) - 1: 0}
    return pl.pallas_call(
        _combine_ln_kernel,
        grid=(n // tm,),
        in_specs=in_specs,
        out_specs=row_x,
        out_shape=jax.ShapeDtypeStruct((t, d), F32),
        input_output_aliases=aliases,
        compiler_params=_cparams(("parallel",)),
    )(*args)


def _moe(x1, x1b, route, row0, n_tok, prev_out, layer, w1, w3, w2, gain, bias):
    t = n_tok
    n_assign = 2 * t
    take = lambda a, idx: a.at[idx].get(mode='promise_in_bounds')
    route = lax.slice_in_dim(route, row0, row0 + t, axis=0)
    gate = route[:, 0:2].reshape(n_assign)
    e = route[:, 2:4].astype(jnp.int32).reshape(n_assign)
    order = jnp.argsort(e).astype(jnp.int32)
    rank = jnp.argsort(order).astype(jnp.int32)
    experts = jnp.arange(N_EXPERTS, dtype=jnp.int32)
    counts = jnp.sum((e[:, None] == experts[None, :]).astype(jnp.int32), axis=0)
    padded = (counts + MOE_ROWS - 1) // MOE_ROWS * MOE_ROWS
    starts = jnp.cumsum(counts) - counts
    pends = jnp.cumsum(padded)
    pstarts = pends - padded
    pos = (take(pstarts - starts, e) + rank).reshape(t, 2)
    n_rows = n_assign + N_EXPERTS * MOE_ROWS
    n_blk = n_rows // MOE_ROWS
    blk_row0 = jnp.arange(n_blk, dtype=jnp.int32) * MOE_ROWS
    blk_exp = jnp.minimum(jnp.sum((pends[None, :] <= blk_row0[:, None]).astype(jnp.int32), axis=1), N_EXPERTS - 1)
    n_used = (pends[N_EXPERTS - 1] // MOE_ROWS).astype(jnp.int32)[None]
    base = jnp.repeat(take(starts - pstarts, blk_exp), MOE_ROWS)
    limit = jnp.repeat(take(starts + counts, blk_exp), MOE_ROWS)
    sidx = jnp.arange(n_rows, dtype=jnp.int32) + base
    valid = sidx < limit
    a_src = take(order, jnp.clip(sidx, 0, n_assign - 1))
    src_tok = jnp.where(valid, a_src // 2, 0) + row0
    gw = jnp.where(valid, take(gate, a_src), 0.0)
    xs = take(x1b, src_tok)
    y = _moe_mlp(xs, gw[:, None], blk_exp, n_used, layer, w1, w3, w2)
    ya = take(y, pos[:, 0])
    yb = take(y, pos[:, 1])
    return _combine_ln(x1, row0, ya, yb, gain, bias, prev_out)


def _col_scale(width, n_scaled, scale):
    return jnp.where(jnp.arange(width) < n_scaled, scale, 1.0).astype(F32)[None, :]


def _trunk(x, batch_shapes, rel_bias_table, a_w_in, a_w_out, a_lambda_q1, a_lambda_k1, a_lambda_q2, a_lambda_k2,
           a_subln_g, b_w_in, b_w_out, ln_gain, ln_bias, router_group, router_expert,
           expert_w1, expert_w3, expert_w2):
    a_cols = rel_bias_table[:, :A_HEADS].T
    bias_tiles = _diff_bias_tiles(a_cols)
    a_scale = _col_scale(3 * A_QK_WIDTH, A_QK_WIDTH, A_HEAD_DIM ** -0.5 * LOG2E)
    b_scale = _col_scale(3 * B_GROUP_WIDTH, B_GROUP_WIDTH, B_HEAD_DIM ** -0.5 * LOG2E)
    offsets = []
    start = 0
    for (b, s) in batch_shapes:
        offsets.append((start, b, s))
        start += b * s
    t = x.shape[0]

    for i in range(DEPTH):
        li = i // N_MIXERS
        w_route = jnp.concatenate(
            [router_group[i], router_expert[i],
             jnp.zeros((D_MODEL, LANES - N_GROUPS - N_EXPERTS), F32)], axis=1).astype(BF16)
        gain1, bias1 = ln_gain[i, 0][None, :], ln_bias[i, 0][None, :]
        gain2, bias2 = ln_gain[i, 1][None, :], ln_bias[i, 1][None, :]
        if i % N_MIXERS == 0:
            lambda_init = 0.8 - 0.6 * math.exp(-0.3 * i)
            qkv = _project(x, a_w_in[li].astype(BF16), a_scale)
            lam_vecs = jnp.stack([a_lambda_q1[li], a_lambda_k1[li], a_lambda_q2[li], a_lambda_k2[li]]).astype(F32)
            subg_col = a_subln_g[li].astype(F32)[:, None]
            o = None
            for (st, b, s) in offsets:
                o = _diff_attention(qkv, st, b, s, o, bias_tiles, lam_vecs, subg_col, lambda_init)
            mixer_outs = [o]
            kern = _post_diff_kernel
            w_out = a_w_out[li].astype(BF16)
            scratch_widths = ()
        else:
            outs, lses = [], []
            for g, (window, dil) in enumerate(B_CONFIGS):
                c0 = A_HEADS + g * B_HEADS
                cols = rel_bias_table[:, c0:c0 + B_HEADS].T
                w_g = jnp.concatenate([b_w_in[li][:, (sec * B_GROUPS + g) * B_GROUP_WIDTH:
                                                  (sec * B_GROUPS + g + 1) * B_GROUP_WIDTH] for sec in range(3)],
                                      axis=1).astype(BF16)
                qkv_g = _project_folded(x, w_g, b_scale, dil)
                pair = None
                for (st, b, s) in offsets:
                    pair = _dilated_group(qkv_g, st, b, s, pair, dil, window // (2 * dil), cols)
                outs.append(pair[0])
                lses.append(pair[1])
            mixer_outs = outs + lses
            kern = _post_dilated_kernel
            w_out = b_w_out[li].astype(BF16)
            scratch_widths = (B_GROUP_WIDTH,) * (2 * B_GROUPS)
        x1, x1b, route = _post_mixer_call(kern, x, mixer_outs, w_out, gain1, bias1, w_route, scratch_widths)
        x = _moe(x1, x1b, route, 0, t, None, i, expert_w1, expert_w3, expert_w2, gain2, bias2)
    return x


def kernel(x_prompt, x_sample, rel_bias_table, a_w_in, a_w_out, a_lambda_q1, a_lambda_k1, a_lambda_q2, a_lambda_k2, a_subln_g, b_w_in, b_w_out, ln_gain, ln_bias, router_group, router_expert, expert_w1, expert_w3, expert_w2):
    shapes = [x_prompt.shape[:2], x_sample.shape[:2]]
    x = jnp.concatenate([x_prompt.reshape(-1, D_MODEL), x_sample.reshape(-1, D_MODEL)], axis=0)
    y = _trunk(x, shapes, rel_bias_table, a_w_in, a_w_out, a_lambda_q1, a_lambda_k1, a_lambda_q2, a_lambda_k2,
               a_subln_g, b_w_in, b_w_out, ln_gain, ln_bias, router_group, router_expert,
               expert_w1, expert_w3, expert_w2)
    n_prompt = x_prompt.shape[0] * x_prompt.shape[1]
    return (y[:n_prompt].reshape(x_prompt.shape), y[n_prompt:].reshape(x_sample.shape))
```

```python
import functools
import math

import jax
import jax.numpy as jnp
from jax import lax
from jax.experimental import pallas as pl
from jax.experimental.pallas import tpu as pltpu

F32 = jnp.float32
BF16 = jnp.bfloat16

D_MODEL = 1024
DEPTH = 4
N_MIXERS = 2

A_HEADS = 8
A_HEAD_DIM = 64
A_QK_WIDTH = A_HEADS * 2 * A_HEAD_DIM
A_V_DIM = 2 * A_HEAD_DIM

B_CONFIGS = ((128, 1), (512, 4), (2048, 16))
B_GROUPS = 3
B_HEADS = 8
B_HEAD_DIM = 64
B_GROUP_WIDTH = B_HEADS * B_HEAD_DIM
B_PROJ_WIDTH = B_GROUPS * B_GROUP_WIDTH

NUM_BUCKETS = 32
MAX_DISTANCE = 1024

N_GROUPS = 4
EXPERTS_PER_GROUP = 8
N_EXPERTS = N_GROUPS * EXPERTS_PER_GROUP
D_EXPERT = 512

ALPHA = (2.0 * DEPTH) ** 0.25
LN_EPS = 1e-5
NEG = -1e30
LOG2E = math.log2(math.e)

LANES = 128
BF16_SUBLANES = 16
MOE_ROWS = 256
BIAS_TILE = 256
BIAS_TILE_SPAN = 4
VT_ROWS = A_V_DIM + BF16_SUBLANES
VMEM_LIMIT = 56 * 1024 * 1024


def _cparams(sem):
    return pltpu.CompilerParams(dimension_semantics=sem, vmem_limit_bytes=VMEM_LIMIT)


def _rel_bucket(rel):
    nb = NUM_BUCKETS // 2
    max_exact = nb // 2
    n = jnp.abs(rel)
    side = jnp.where(rel > 0, nb, 0)
    nf = jnp.maximum(n, 1).astype(F32)
    large = max_exact + (jnp.log(nf / max_exact) / math.log(MAX_DISTANCE / max_exact)
                         * (nb - max_exact)).astype(jnp.int32)
    large = jnp.minimum(large, nb - 1)
    return side + jnp.where(n < max_exact, n, large)


def _bias_lookup(cols, rel):
    bucket = _rel_bucket(rel)[None]
    shape = (cols.shape[0],) + (1,) * rel.ndim
    out = jnp.zeros((cols.shape[0],) + rel.shape, F32)
    for b in range(NUM_BUCKETS):
        out = jnp.where(bucket == b, cols[:, b].astype(F32).reshape(shape), out)
    return out


def _proj_kernel(x_ref, w_ref, s_ref, o_ref):
    acc = jnp.dot(x_ref[...].astype(BF16), w_ref[...], preferred_element_type=F32)
    o_ref[...] = (acc * s_ref[...]).astype(o_ref.dtype)


def _project(x, w, col_scale, tm=1024, tn=512):
    m, k = x.shape
    n = w.shape[1]
    return pl.pallas_call(
        _proj_kernel,
        grid=(m // tm, n // tn),
        in_specs=[pl.BlockSpec((tm, k), lambda i, j: (i, 0)),
                  pl.BlockSpec((k, tn), lambda i, j: (0, j)),
                  pl.BlockSpec((1, tn), lambda i, j: (0, j))],
        out_specs=pl.BlockSpec((tm, tn), lambda i, j: (i, j)),
        out_shape=jax.ShapeDtypeStruct((m, n), BF16),
        compiler_params=_cparams(("parallel", "arbitrary")),
    )(x, w, col_scale)


def _proj_folded_kernel(x_ref, w_ref, s_ref, o_ref, y_ref, *, fold):
    y = jnp.dot(x_ref[...].astype(BF16), w_ref[...], preferred_element_type=F32) * s_ref[...]
    if fold == 1:
        o_ref[...] = y.astype(o_ref.dtype)
        return
    n_tiles, rows, _ = y_ref.shape
    for c in range(n_tiles):
        y_ref[c] = y[:, c * LANES:(c + 1) * LANES]
    for r in range(fold):
        for c in range(n_tiles):
            col = (r * n_tiles + c) * LANES
            o_ref[:, col:col + LANES] = y_ref[c, pl.ds(r, rows // fold, stride=fold), :].astype(o_ref.dtype)


def _project_folded(x, w, col_scale, fold, tm=1024):
    m, k = x.shape
    n = w.shape[1]
    assert tm % (fold * BF16_SUBLANES) == 0 and m % tm == 0
    return pl.pallas_call(
        functools.partial(_proj_folded_kernel, fold=fold),
        grid=(m // tm,),
        in_specs=[pl.BlockSpec((tm, k), lambda i: (i, 0)),
                  pl.BlockSpec((k, n), lambda i: (0, 0)),
                  pl.BlockSpec((1, n), lambda i: (0, 0))],
        out_specs=pl.BlockSpec((tm // fold, fold * n), lambda i: (i, 0)),
        out_shape=jax.ShapeDtypeStruct((m // fold, fold * n), BF16),
        scratch_shapes=[pltpu.VMEM((n // LANES, tm, LANES), F32)],
        compiler_params=_cparams(("parallel",)),
    )(x, w, col_scale)


def _diff_attn_kernel(lam_ref, subg_ref, q_ref, k_ref, vt_ref, bt_ref, *rest, qb, kb, nk, lambda_init, aliased):
    o_ref, wq_ref, s_ref, acc_ref, m_ref = rest[1:] if aliased else rest
    qi = pl.program_id(2)
    q = q_ref[...]
    lane = lax.broadcasted_iota(jnp.int32, q.shape, 1)
    zero = jnp.zeros_like(q)
    wq_ref[...] = jnp.concatenate([jnp.where(lane < A_HEAD_DIM, q, zero),
                                   jnp.where(lane >= A_HEAD_DIM, q, zero)], axis=0)
    m_ref[...] = jnp.full(m_ref.shape, NEG, F32)
    acc_ref[...] = jnp.zeros(acc_ref.shape, F32)
    tpc = kb // BIAS_TILE
    qt = qb // BIAS_TILE
    reps = 2 * qt
    span = BIAS_TILE_SPAN
    q_tile0 = qi * qt

    jb0 = jnp.maximum((q_tile0 - (span - 1)) // tpc, 0)
    jb1 = jnp.minimum((q_tile0 + qt - 1 + span - 1) // tpc + 1, nk)
    n_true_band = jb1 - jb0
    far_pairs = (nk - n_true_band) // 2
    n_band = nk - 2 * far_pairs
    leftover = jnp.where(jb1 < nk, nk - 1, jb0 - 1)
    off_left = jnp.concatenate([bt_ref[0][0:1, :]] * reps, axis=1)
    off_right = jnp.concatenate([bt_ref[2 * span][0:1, :]] * reps, axis=1)
    off_zero = jnp.zeros_like(off_left)

    def band_chunk(v):
        return jnp.where(v < n_true_band, jb0 + v, leftover)

    def far_chunk(w):
        return jnp.where(w < jb0, w, w - jb0 + jb1)

    def far_off(w):
        return jnp.where(w < jb0, off_left, off_right)

    def logits(kc):
        k0 = pl.multiple_of(kc * kb, kb)
        return lax.dot_general(k_ref[pl.ds(k0, kb), :], wq_ref[...], (((1,), (1,)), ((), ())),
                               preferred_element_type=F32)

    def scores_band(kc, slot):
        s = logits(kc)
        mloc = None
        for a in range(tpc):
            tiles = [bt_ref[jnp.clip(kc * tpc + a - (q_tile0 + b), -span, span) + span] for b in range(qt)]
            rows = slice(a * BIAS_TILE, (a + 1) * BIAS_TILE)
            sa = s[rows, :] + jnp.concatenate(tiles * 2, axis=1)
            s_ref[slot, rows, :] = sa
            ma = jnp.max(sa, axis=0, keepdims=True)
            mloc = ma if mloc is None else jnp.maximum(mloc, ma)
        return mloc

    def scores_far(kc, slot, off):
        s = logits(kc)
        s_ref[slot] = s
        return jnp.max(s, axis=0, keepdims=True) + off

    def update(kc, slot, mloc, off):
        m_old = m_ref[...]
        m_new = jnp.maximum(m_old, mloc)
        alpha = jnp.exp2(m_old - m_new)
        p = jnp.exp2(s_ref[slot] - (m_new - off))
        pv = jnp.dot(vt_ref[kc], p.astype(BF16), preferred_element_type=F32)
        acc_ref[...] = acc_ref[...] * alpha + pv
        m_ref[...] = m_new

    def band_pair(j, carry):
        mloc_p, kc_p = carry
        k1, k2 = band_chunk(2 * j + 1), band_chunk(2 * j + 2)
        m1 = scores_band(k1, 1)
        update(kc_p, 0, mloc_p, off_zero)
        m2 = scores_band(k2, 0)
        update(k1, 1, m1, off_zero)
        return m2, k2

    def far_pair(j, carry):
        mloc_p, kc_p, off_p = carry
        k1, k2 = far_chunk(2 * j), far_chunk(2 * j + 1)
        o1, o2 = far_off(2 * j), far_off(2 * j + 1)
        m1 = scores_far(k1, 1, o1)
        update(kc_p, 0, mloc_p, off_p)
        m2 = scores_far(k2, 0, o2)
        update(k1, 1, m1, o1)
        return m2, k2, o2

    kc0 = band_chunk(0)
    mloc_p, kc_p = lax.fori_loop(0, n_band // 2 - 1, band_pair, (scores_band(kc0, 0), kc0))
    mloc_p, kc_p, off_p = lax.fori_loop(0, far_pairs, far_pair, (mloc_p, kc_p, off_zero))
    kc_t = band_chunk(n_band - 1)
    mloc_t = scores_band(kc_t, 1)
    update(kc_p, 0, mloc_p, off_p)
    update(kc_t, 1, mloc_t, off_zero)

    lamv = lam_ref[...]
    lam = (jnp.exp(jnp.sum(lamv[0:1] * lamv[1:2], axis=1, keepdims=True))
           - jnp.exp(jnp.sum(lamv[2:3] * lamv[3:4], axis=1, keepdims=True)) + lambda_init)
    acc = acc_ref[...]
    on = acc[:A_V_DIM] / acc[A_V_DIM:A_V_DIM + 1]
    o = on[:, :qb] - lam * on[:, qb:]
    o = o * lax.rsqrt(jnp.mean(o * o, axis=0, keepdims=True) + LN_EPS)
    o = o * (subg_ref[...] * (1.0 - lambda_init))
    o_ref[...] = o.T.astype(o_ref.dtype)


def _diff_attention(qkv, row0, b, s, prev_out, bias_tiles, lam_vecs, subg_col, lambda_init, qb=256, kb=1024):
    t = qkv.shape[0]
    kb = min(kb, s // 2)
    nk = s // kb
    assert qb % BIAS_TILE == 0 and kb % BIAS_TILE == 0 and nk % 2 == 0 and nk * kb == s
    assert row0 % s == 0 and s % qb == 0
    v = qkv[row0:row0 + b * s, 2 * A_QK_WIDTH:].reshape(b, nk, kb, A_HEADS, A_V_DIM).transpose(0, 3, 1, 4, 2)
    ones = jnp.ones((b, A_HEADS, nk, 1, kb), BF16)
    pad = jnp.zeros((b, A_HEADS, nk, VT_ROWS - A_V_DIM - 1, kb), BF16)
    vt = jnp.concatenate([v, ones, pad], axis=3)
    aliased = prev_out is not None
    kern = functools.partial(_diff_attn_kernel, qb=qb, kb=kb, nk=nk, lambda_init=lambda_init, aliased=aliased)
    n_tiles = 2 * BIAS_TILE_SPAN + 1
    q_blk0, k_blk0 = row0 // qb, row0 // s
    q_blocks = s // qb
    in_specs = [pl.BlockSpec((4, A_HEAD_DIM), lambda bi, h, qi: (0, 0)),
                pl.BlockSpec((A_V_DIM, 1), lambda bi, h, qi: (0, 0)),
                pl.BlockSpec((qb, LANES), lambda bi, h, qi: (q_blk0 + bi * q_blocks + qi, h)),
                pl.BlockSpec((s, LANES), lambda bi, h, qi: (k_blk0 + bi, A_HEADS + h)),
                pl.BlockSpec((None, None, nk, VT_ROWS, kb), lambda bi, h, qi: (bi, h, 0, 0, 0)),
                pl.BlockSpec((None, n_tiles, BIAS_TILE, BIAS_TILE), lambda bi, h, qi: (h, 0, 0, 0))]
    args = [lam_vecs, subg_col, qkv, qkv, vt, bias_tiles]
    if aliased:
        in_specs.append(pl.BlockSpec(memory_space=pl.ANY))
        args.append(prev_out)
    return pl.pallas_call(
        kern,
        grid=(b, A_HEADS, q_blocks),
        in_specs=in_specs,
        out_specs=pl.BlockSpec((qb, LANES), lambda bi, h, qi: (q_blk0 + bi * q_blocks + qi, h)),
        out_shape=jax.ShapeDtypeStruct((t, A_HEADS * A_V_DIM), BF16),
        scratch_shapes=[pltpu.VMEM((2 * qb, LANES), BF16),
                        pltpu.VMEM((2, kb, 2 * qb), F32),
                        pltpu.VMEM((VT_ROWS, 2 * qb), F32),
                        pltpu.VMEM((1, 2 * qb), F32)],
        input_output_aliases={len(args) - 1: 0} if aliased else {},
        compiler_params=_cparams(("parallel", "parallel", "arbitrary")),
    )(*args)


def _diff_bias_tiles(a_cols):
    t = BIAS_TILE
    e = (jnp.arange(2 * BIAS_TILE_SPAN + 1, dtype=jnp.int32) - BIAS_TILE_SPAN) * t
    j = jnp.arange(t, dtype=jnp.int32)
    rel = e[:, None, None] + j[None, :, None] - j[None, None, :]
    return _bias_lookup(a_cols, rel) * LOG2E


def _dilated_kernel(q_ref, kp_ref, kc_ref, kn_ref, vp_ref, vc_ref, vn_ref, bias_ref, *rest,
                    lb, sb, radius, n_blk, aliased):
    o_ref, lse_ref = rest[2:] if aliased else rest
    li = pl.program_id(2)
    kwin = jnp.concatenate([kp_ref[lb - radius:, :], kc_ref[...], kn_ref[:radius, :]], axis=0)
    vwin = jnp.concatenate([vp_ref[lb - radius:, :], vc_ref[...], vn_ref[:radius, :]], axis=0)
    q = q_ref[...]
    swin = sb + 2 * radius
    jj = lax.broadcasted_iota(jnp.int32, (swin, sb), 0)
    for u in range(lb // sb):
        rows = slice(u * sb, u * sb + swin)
        valid = jnp.logical_and(jnp.logical_or(li > 0, jj + u * sb >= radius),
                                jnp.logical_or(li < n_blk - 1, jj + u * sb < lb + radius))
        outs, lses = [], []
        for h in range(B_HEADS):
            cs = slice(h * B_HEAD_DIM, (h + 1) * B_HEAD_DIM)
            s = lax.dot_general(kwin[rows, cs], q[u * sb:(u + 1) * sb, cs], (((1,), (1,)), ((), ())),
                                preferred_element_type=F32)
            s = jnp.where(valid, s + bias_ref[h], NEG)
            m = jnp.max(s, axis=0, keepdims=True)
            p = jnp.exp2(s - m)
            l = jnp.sum(p, axis=0, keepdims=True)
            o = lax.dot_general(vwin[rows, cs], p.astype(BF16), (((0,), (0,)), ((), ())),
                                preferred_element_type=F32)
            outs.append(o / l)
            lses.append(jnp.broadcast_to(m + jnp.log2(l), (B_HEAD_DIM, sb)))
        o_ref[u * sb:(u + 1) * sb, :] = jnp.concatenate(outs, axis=0).T
        lse_ref[u * sb:(u + 1) * sb, :] = jnp.concatenate(lses, axis=0).T


def _dilated_group(x, row0, b, s, prev, dilation, radius, cols):
    t = x.shape[0] * dilation
    n_col = 3
    assert x.shape[1] == dilation * n_col * B_GROUP_WIDTH
    sub_len = s // dilation
    lb = min(512, sub_len)
    sb = lb
    n_blk = sub_len // lb
    win = sb + 2 * radius
    assert row0 % (dilation * lb) == 0 and n_blk * lb == sub_len and lb % sb == 0
    blk0 = row0 // dilation // lb
    i = jnp.arange(sb, dtype=jnp.int32)[None, :]
    j = jnp.arange(win, dtype=jnp.int32)[:, None]
    rel_sub = j - radius - i
    bias = _bias_lookup(cols, rel_sub * dilation) * LOG2E
    bias = jnp.where((jnp.abs(rel_sub) <= radius)[None], bias, NEG)

    def spec(col, shift):
        def index(bi, r, li):
            return (blk0 + bi * n_blk + jnp.clip(li + shift, 0, n_blk - 1), r * n_col + col)
        return pl.BlockSpec((lb, B_GROUP_WIDTH), index)

    aliased = prev is not None
    kern = functools.partial(_dilated_kernel, lb=lb, sb=sb, radius=radius, n_blk=n_blk, aliased=aliased)
    out_spec = pl.BlockSpec((lb, B_GROUP_WIDTH), lambda bi, r, li: (blk0 + bi * n_blk + li, r))
    shape = jax.ShapeDtypeStruct((t // dilation, dilation * B_GROUP_WIDTH), F32)
    in_specs = [spec(0, 0),
                spec(1, -1), spec(1, 0), spec(1, 1),
                spec(2, -1), spec(2, 0), spec(2, 1),
                pl.BlockSpec((B_HEADS, win, sb), lambda bi, r, li: (0, 0, 0))]
    args = [x, x, x, x, x, x, x, bias]
    aliases = {}
    if aliased:
        in_specs += [pl.BlockSpec(memory_space=pl.ANY)] * 2
        args += list(prev)
        aliases = {len(args) - 2: 0, len(args) - 1: 1}
    return pl.pallas_call(
        kern,
        grid=(b, dilation, n_blk),
        in_specs=in_specs,
        out_specs=[out_spec, out_spec],
        out_shape=[shape, shape],
        input_output_aliases=aliases,
        compiler_params=_cparams(("parallel", "parallel", "arbitrary")),
    )(*args)


def _layer_norm(z, g, b):
    mu = jnp.mean(z, axis=-1, keepdims=True)
    zc = z - mu
    var = jnp.mean(zc * zc, axis=-1, keepdims=True)
    return zc * lax.rsqrt(var + LN_EPS) * g + b


def _route(logits):
    lane = lax.broadcasted_iota(jnp.int32, logits.shape, 1).astype(F32)
    far = float(LANES)
    is_group = lane < N_GROUPS
    gl = jnp.where(is_group, logits, NEG)
    gmax = jnp.max(gl, axis=1, keepdims=True)
    gidx = jnp.min(jnp.where(gl == gmax, lane, far), axis=1, keepdims=True)
    gden = jnp.sum(jnp.where(is_group, jnp.exp(gl - gmax), 0.0), axis=1, keepdims=True)
    g_w = 1.0 / gden
    lo = N_GROUPS + EXPERTS_PER_GROUP * gidx
    in_group = jnp.logical_and(lane >= lo, lane < lo + EXPERTS_PER_GROUP)
    el = jnp.where(in_group, logits, NEG)
    v1 = jnp.max(el, axis=1, keepdims=True)
    i1 = jnp.min(jnp.where(el == v1, lane, far), axis=1, keepdims=True)
    el2 = jnp.where(lane == i1, NEG, el)
    v2 = jnp.max(el2, axis=1, keepdims=True)
    i2 = jnp.min(jnp.where(el2 == v2, lane, far), axis=1, keepdims=True)
    e2 = jnp.exp(v2 - v1)
    den = 1.0 + e2
    gate0 = g_w / den
    gate1 = g_w * e2 / den
    out = jnp.where(lane == 0.0, gate0, 0.0)
    out = jnp.where(lane == 1.0, gate1, out)
    out = jnp.where(lane == 2.0, i1 - N_GROUPS, out)
    out = jnp.where(lane == 3.0, i2 - N_GROUPS, out)
    return out


def _post_mixer(x_ref, y, w_ref, g_ref, b_ref, wr_ref, x1_ref, x1b_ref, route_ref):
    h = jnp.dot(y, w_ref[...], preferred_element_type=F32)
    x1 = _layer_norm(ALPHA * x_ref[...] + h, g_ref[...], b_ref[...])
    x1b = x1.astype(BF16)
    x1_ref[...] = x1
    x1b_ref[...] = x1b
    route_ref[...] = _route(jnp.dot(x1b, wr_ref[...], preferred_element_type=F32))


def _post_diff_kernel(x_ref, o_ref, w_ref, g_ref, b_ref, wr_ref, x1_ref, x1b_ref, route_ref):
    _post_mixer(x_ref, o_ref[...], w_ref, g_ref, b_ref, wr_ref, x1_ref, x1b_ref, route_ref)


def _unfold(src_ref, scratch_ref, fold):
    if fold == 1:
        return src_ref[...]
    n_tiles, rows, _ = scratch_ref.shape
    for r in range(fold):
        for c in range(n_tiles):
            col = (r * n_tiles + c) * LANES
            scratch_ref[c, pl.ds(r, rows // fold, stride=fold), :] = src_ref[:, col:col + LANES]
    return jnp.concatenate([scratch_ref[c] for c in range(n_tiles)], axis=1)


def _post_dilated_kernel(x_ref, o0_ref, o1_ref, o2_ref, l0_ref, l1_ref, l2_ref, w_ref, g_ref, b_ref, wr_ref,
                         x1_ref, x1b_ref, route_ref, *scratch):
    folds = [dil for (_, dil) in B_CONFIGS]
    o_refs, l_refs = (o0_ref, o1_ref, o2_ref), (l0_ref, l1_ref, l2_ref)
    o0, o1, o2 = [_unfold(o_refs[g], scratch[g], folds[g]) for g in range(B_GROUPS)]
    l0, l1, l2 = [_unfold(l_refs[g], scratch[B_GROUPS + g], folds[g]) for g in range(B_GROUPS)]
    m = jnp.maximum(jnp.maximum(l0, l1), l2)
    e0, e1, e2 = jnp.exp2(l0 - m), jnp.exp2(l1 - m), jnp.exp2(l2 - m)
    den = e0 + e1 + e2
    o = (o0 * (e0 / den) + o1 * (e1 / den)) + o2 * (e2 / den)
    _post_mixer(x_ref, o.astype(BF16), w_ref, g_ref, b_ref, wr_ref, x1_ref, x1b_ref, route_ref)


def _post_mixer_call(kern, x, mixer_outs, w_out, gain, bias, w_route, scratch_widths=(), tm=512):
    t, d = x.shape
    row = lambda width: pl.BlockSpec((tm, width), lambda i: (i, 0))
    folded = lambda a: pl.BlockSpec((tm * a.shape[0] // t, a.shape[1]), lambda i: (i, 0))
    whole = lambda a: pl.BlockSpec(a.shape, lambda i: (0,) * a.ndim)
    return pl.pallas_call(
        kern,
        grid=(t // tm,),
        in_specs=[row(d)] + [folded(a) for a in mixer_outs]
        + [whole(w_out), whole(gain), whole(bias), whole(w_route)],
        out_specs=[row(d), row(d), row(LANES)],
        out_shape=[jax.ShapeDtypeStruct((t, d), F32), jax.ShapeDtypeStruct((t, d), BF16),
                   jax.ShapeDtypeStruct((t, LANES), F32)],
        scratch_shapes=[pltpu.VMEM((width // LANES, tm, LANES), F32) for width in scratch_widths],
        compiler_params=_cparams(("parallel",)),
    )(x, *mixer_outs, w_out, gain, bias, w_route)


def _moe_mlp_kernel(blk_ref, used_ref, src_ref, x_hbm, gw_ref, w1_ref, w3_ref, w2_ref, y_ref,
                    w1b_ref, w3b_ref, w2b_ref, xbuf_ref, sem_ref):
    i = pl.program_id(0)
    n_used = used_ref[0]
    active = i < n_used
    slot = lax.rem(i, 2)
    new_expert = jnp.logical_or(i == 0, blk_ref[i] != blk_ref[jnp.maximum(i - 1, 0)])

    def row_copy(block, r, dst_slot):
        tok = src_ref[block * MOE_ROWS + r]
        return pltpu.make_async_copy(x_hbm.at[pl.ds(tok, 1), :], xbuf_ref.at[dst_slot, pl.ds(r, 1), :],
                                     sem_ref.at[dst_slot])

    def start_rows(block, dst_slot):
        for r in range(MOE_ROWS):
            row_copy(block, r, dst_slot).start()

    def wait_rows(block, dst_slot):
        for r in range(MOE_ROWS):
            row_copy(block, r, dst_slot).wait()

    @pl.when(i == 0)
    def _():
        start_rows(0, 0)

    @pl.when(jnp.logical_and(active, new_expert))
    def _():
        w1b_ref[...] = w1_ref[...].astype(BF16)
        w3b_ref[...] = w3_ref[...].astype(BF16)
        w2b_ref[...] = w2_ref[...].astype(BF16)

    @pl.when(active)
    def _():
        start_rows(i + 1, 1 - slot)
        wait_rows(i, slot)
        x = xbuf_ref[slot].astype(BF16)
        h1 = jnp.dot(x, w1b_ref[...], preferred_element_type=F32)
        h3 = jnp.dot(x, w3b_ref[...], preferred_element_type=F32)
        h = (h1 * jax.nn.sigmoid(h1)) * h3
        y = jnp.dot(h.astype(BF16), w2b_ref[...], preferred_element_type=F32)
        y_ref[...] = (y * gw_ref[...]).astype(y_ref.dtype)

    @pl.when(jnp.logical_not(active))
    def _():
        y_ref[...] = jnp.zeros(y_ref.shape, y_ref.dtype)

    @pl.when(i == n_used)
    def _():
        wait_rows(i, slot)


def _moe_mlp(x, src_tok, gw, blk_exp, n_used, layer, w1, w3, w2):
    p = src_tok.shape[0]
    d = x.shape[1]
    n_blk = p // MOE_ROWS
    grid_spec = pltpu.PrefetchScalarGridSpec(
        num_scalar_prefetch=3,
        grid=(n_blk,),
        in_specs=[pl.BlockSpec(memory_space=pl.ANY),
                  pl.BlockSpec((MOE_ROWS, 1), lambda i, be, nu, st: (i, 0)),
                  pl.BlockSpec((None, None, d, D_EXPERT), lambda i, be, nu, st: (layer, be[i], 0, 0)),
                  pl.BlockSpec((None, None, d, D_EXPERT), lambda i, be, nu, st: (layer, be[i], 0, 0)),
                  pl.BlockSpec((None, None, D_EXPERT, d), lambda i, be, nu, st: (layer, be[i], 0, 0))],
        out_specs=pl.BlockSpec((MOE_ROWS, d), lambda i, be, nu, st: (i, 0)),
        scratch_shapes=[pltpu.VMEM((d, D_EXPERT), BF16), pltpu.VMEM((d, D_EXPERT), BF16),
                        pltpu.VMEM((D_EXPERT, d), BF16),
                        pltpu.VMEM((2, MOE_ROWS, d), F32),
                        pltpu.SemaphoreType.DMA((2,))],
    )
    return pl.pallas_call(
        _moe_mlp_kernel,
        grid_spec=grid_spec,
        out_shape=jax.ShapeDtypeStruct((p, d), BF16),
        compiler_params=_cparams(("arbitrary",)),
    )(blk_exp, n_used, src_tok, x, gw, w1, w3, w2)


def _combine_ln_kernel(x_ref, ya_ref, yb_ref, g_ref, b_ref, *rest):
    o_ref = rest[-1]
    z = ALPHA * x_ref[...] + (ya_ref[...].astype(F32) + yb_ref[...].astype(F32))
    o_ref[...] = _layer_norm(z, g_ref[...], b_ref[...])


def _combine_ln(x1, row0, ya, yb, gain, bias, prev_out, tm=1024):
    t, d = x1.shape
    n = ya.shape[0]
    assert row0 % tm == 0 and n % tm == 0
    blk0 = row0 // tm
    row_x = pl.BlockSpec((tm, d), lambda i: (blk0 + i, 0))
    row_y = pl.BlockSpec((tm, d), lambda i: (i, 0))
    vec = pl.BlockSpec((1, d), lambda i: (0, 0))
    in_specs = [row_x, row_y, row_y, vec, vec]
    args = [x1, ya, yb, gain, bias]
    aliases = {}
    if prev_out is not None:
        in_specs.append(pl.BlockSpec(memory_space=pl.ANY))
        args.append(prev_out)
        aliases = {len(args) - 1: 0}
    return pl.pallas_call(
        _combine_ln_kernel,
        grid=(n // tm,),
        in_specs=in_specs,
        out_specs=row_x,
        out_shape=jax.ShapeDtypeStruct((t, d), F32),
        input_output_aliases=aliases,
        compiler_params=_cparams(("parallel",)),
    )(*args)


def _moe(x1, x1b, route, row0, n_tok, prev_out, layer, w1, w3, w2, gain, bias):
    t = n_tok
    n_assign = 2 * t
    take = lambda a, idx: a.at[idx].get(mode='promise_in_bounds')
    route = lax.slice_in_dim(route, row0, row0 + t, axis=0)
    gate = route[:, 0:2].reshape(n_assign)
    e = route[:, 2:4].astype(jnp.int32).reshape(n_assign)
    order = jnp.argsort(e).astype(jnp.int32)
    rank = jnp.argsort(order).astype(jnp.int32)
    experts = jnp.arange(N_EXPERTS, dtype=jnp.int32)
    counts = jnp.sum((e[:, None] == experts[None, :]).astype(jnp.int32), axis=0)
    padded = (counts + MOE_ROWS - 1) // MOE_ROWS * MOE_ROWS
    starts = jnp.cumsum(counts) - counts
    pends = jnp.cumsum(padded)
    pstarts = pends - padded
    pos = (take(pstarts - starts, e) + rank).reshape(t, 2)
    n_rows = n_assign + N_EXPERTS * MOE_ROWS
    n_blk = n_rows // MOE_ROWS
    blk_row0 = jnp.arange(n_blk, dtype=jnp.int32) * MOE_ROWS
    blk_exp = jnp.minimum(jnp.sum((pends[None, :] <= blk_row0[:, None]).astype(jnp.int32), axis=1), N_EXPERTS - 1)
    n_used = (pends[N_EXPERTS - 1] // MOE_ROWS).astype(jnp.int32)[None]
    base = jnp.repeat(take(starts - pstarts, blk_exp), MOE_ROWS)
    limit = jnp.repeat(take(starts + counts, blk_exp), MOE_ROWS)
    sidx = jnp.arange(n_rows, dtype=jnp.int32) + base
    valid = sidx < limit
    a_src = take(order, jnp.clip(sidx, 0, n_assign - 1))
    src_tok = jnp.where(valid, a_src // 2, 0) + row0
    gw = jnp.where(valid, take(gate, a_src), 0.0)
    y = _moe_mlp(x1, src_tok, gw[:, None], blk_exp, n_used, layer, w1, w3, w2)
    ya = take(y, pos[:, 0])
    yb = take(y, pos[:, 1])
    return _combine_ln(x1, row0, ya, yb, gain, bias, prev_out)


def _col_scale(width, n_scaled, scale):
    return jnp.where(jnp.arange(width) < n_scaled, scale, 1.0).astype(F32)[None, :]


def _trunk(x, batch_shapes, rel_bias_table, a_w_in, a_w_out, a_lambda_q1, a_lambda_k1, a_lambda_q2, a_lambda_k2,
           a_subln_g, b_w_in, b_w_out, ln_gain, ln_bias, router_group, router_expert,
           expert_w1, expert_w3, expert_w2):
    a_cols = rel_bias_table[:, :A_HEADS].T
    bias_tiles = _diff_bias_tiles(a_cols)
    a_scale = _col_scale(3 * A_QK_WIDTH, A_QK_WIDTH, A_HEAD_DIM ** -0.5 * LOG2E)
    b_scale = _col_scale(3 * B_GROUP_WIDTH, B_GROUP_WIDTH, B_HEAD_DIM ** -0.5 * LOG2E)
    offsets = []
    start = 0
    for (b, s) in batch_shapes:
        offsets.append((start, b, s))
        start += b * s
    t = x.shape[0]

    for i in range(DEPTH):
        li = i // N_MIXERS
        w_route = jnp.concatenate(
            [router_group[i], router_expert[i],
             jnp.zeros((D_MODEL, LANES - N_GROUPS - N_EXPERTS), F32)], axis=1).astype(BF16)
        gain1, bias1 = ln_gain[i, 0][None, :], ln_bias[i, 0][None, :]
        gain2, bias2 = ln_gain[i, 1][None, :], ln_bias[i, 1][None, :]
        if i % N_MIXERS == 0:
            lambda_init = 0.8 - 0.6 * math.exp(-0.3 * i)
            qkv = _project(x, a_w_in[li].astype(BF16), a_scale)
            lam_vecs = jnp.stack([a_lambda_q1[li], a_lambda_k1[li], a_lambda_q2[li], a_lambda_k2[li]]).astype(F32)
            subg_col = a_subln_g[li].astype(F32)[:, None]
            o = None
            for (st, b, s) in offsets:
                o = _diff_attention(qkv, st, b, s, o, bias_tiles, lam_vecs, subg_col, lambda_init)
            mixer_outs = [o]
            kern = _post_diff_kernel
            w_out = a_w_out[li].astype(BF16)
            scratch_widths = ()
        else:
            outs, lses = [], []
            for g, (window, dil) in enumerate(B_CONFIGS):
                c0 = A_HEADS + g * B_HEADS
                cols = rel_bias_table[:, c0:c0 + B_HEADS].T
                w_g = jnp.concatenate([b_w_in[li][:, (sec * B_GROUPS + g) * B_GROUP_WIDTH:
                                                  (sec * B_GROUPS + g + 1) * B_GROUP_WIDTH] for sec in range(3)],
                                      axis=1).astype(BF16)
                qkv_g = _project_folded(x, w_g, b_scale, dil)
                pair = None
                for (st, b, s) in offsets:
                    pair = _dilated_group(qkv_g, st, b, s, pair, dil, window // (2 * dil), cols)
                outs.append(pair[0])
                lses.append(pair[1])
            mixer_outs = outs + lses
            kern = _post_dilated_kernel
            w_out = b_w_out[li].astype(BF16)
            scratch_widths = (B_GROUP_WIDTH,) * (2 * B_GROUPS)
        x1, x1b, route = _post_mixer_call(kern, x, mixer_outs, w_out, gain1, bias1, w_route, scratch_widths)
        x = _moe(x1, x1b, route, 0, t, None, i, expert_w1, expert_w3, expert_w2, gain2, bias2)
    return x


def kernel(x_prompt, x_sample, rel_bias_table, a_w_in, a_w_out, a_lambda_q1, a_lambda_k1, a_lambda_q2, a_lambda_k2, a_subln_g, b_w_in, b_w_out, ln_gain, ln_bias, router_group, router_expert, expert_w1, expert_w3, expert_w2):
    shapes = [x_prompt.shape[:2], x_sample.shape[:2]]
    x = jnp.concatenate([x_prompt.reshape(-1, D_MODEL), x_sample.reshape(-1, D_MODEL)], axis=0)
    y = _trunk(x, shapes, rel_bias_table, a_w_in, a_w_out, a_lambda_q1, a_lambda_k1, a_lambda_q2, a_lambda_k2,
               a_subln_g, b_w_in, b_w_out, ln_gain, ln_bias, router_group, router_expert,
               expert_w1, expert_w3, expert_w2)
    n_prompt = x_prompt.shape[0] * x_prompt.shape[1]
    return (y[:n_prompt].reshape(x_prompt.shape), y[n_prompt:].reshape(x_sample.shape))
```

```python
import functools
import math

import jax
import jax.numpy as jnp
from jax import lax
from jax.experimental import pallas as pl
from jax.experimental.pallas import tpu as pltpu

F32 = jnp.float32
BF16 = jnp.bfloat16

D_MODEL = 1024
DEPTH = 4
N_MIXERS = 2

A_HEADS = 8
A_HEAD_DIM = 64
A_QK_WIDTH = A_HEADS * 2 * A_HEAD_DIM
A_V_DIM = 2 * A_HEAD_DIM

B_CONFIGS = ((128, 1), (512, 4), (2048, 16))
B_GROUPS = 3
B_HEADS = 8
B_HEAD_DIM = 64
B_GROUP_WIDTH = B_HEADS * B_HEAD_DIM
B_PROJ_WIDTH = B_GROUPS * B_GROUP_WIDTH

NUM_BUCKETS = 32
MAX_DISTANCE = 1024

N_GROUPS = 4
EXPERTS_PER_GROUP = 8
N_EXPERTS = N_GROUPS * EXPERTS_PER_GROUP
D_EXPERT = 512

ALPHA = (2.0 * DEPTH) ** 0.25
LN_EPS = 1e-5
NEG = -1e30
LOG2E = math.log2(math.e)

LANES = 128
BF16_SUBLANES = 16
MOE_ROWS = 256
BIAS_TILE = 256
BIAS_TILE_SPAN = 4
VT_ROWS = A_V_DIM + BF16_SUBLANES
VMEM_LIMIT = 56 * 1024 * 1024


def _cparams(sem):
    return pltpu.CompilerParams(dimension_semantics=sem, vmem_limit_bytes=VMEM_LIMIT)


def _rel_bucket(rel):
    nb = NUM_BUCKETS // 2
    max_exact = nb // 2
    n = jnp.abs(rel)
    side = jnp.where(rel > 0, nb, 0)
    nf = jnp.maximum(n, 1).astype(F32)
    large = max_exact + (jnp.log(nf / max_exact) / math.log(MAX_DISTANCE / max_exact)
                         * (nb - max_exact)).astype(jnp.int32)
    large = jnp.minimum(large, nb - 1)
    return side + jnp.where(n < max_exact, n, large)


def _bias_lookup(cols, rel):
    bucket = _rel_bucket(rel)[None]
    shape = (cols.shape[0],) + (1,) * rel.ndim
    out = jnp.zeros((cols.shape[0],) + rel.shape, F32)
    for b in range(NUM_BUCKETS):
        out = jnp.where(bucket == b, cols[:, b].astype(F32).reshape(shape), out)
    return out


def _proj_kernel(x_ref, w_ref, s_ref, o_ref):
    acc = jnp.dot(x_ref[...].astype(BF16), w_ref[...], preferred_element_type=F32)
    o_ref[...] = (acc * s_ref[...]).astype(o_ref.dtype)


def _project(x, w, col_scale, tm=1024, tn=512):
    m, k = x.shape
    n = w.shape[1]
    return pl.pallas_call(
        _proj_kernel,
        grid=(m // tm, n // tn),
        in_specs=[pl.BlockSpec((tm, k), lambda i, j: (i, 0)),
                  pl.BlockSpec((k, tn), lambda i, j: (0, j)),
                  pl.BlockSpec((1, tn), lambda i, j: (0, j))],
        out_specs=pl.BlockSpec((tm, tn), lambda i, j: (i, j)),
        out_shape=jax.ShapeDtypeStruct((m, n), BF16),
        compiler_params=_cparams(("parallel", "arbitrary")),
    )(x, w, col_scale)


def _proj_folded_kernel(x_ref, w_ref, s_ref, o_ref, y_ref, *, fold):
    y = jnp.dot(x_ref[...].astype(BF16), w_ref[...], preferred_element_type=F32) * s_ref[...]
    if fold == 1:
        o_ref[...] = y.astype(o_ref.dtype)
        return
    n_tiles, rows, _ = y_ref.shape
    for c in range(n_tiles):
        y_ref[c] = y[:, c * LANES:(c + 1) * LANES]
    for r in range(fold):
        for c in range(n_tiles):
            col = (r * n_tiles + c) * LANES
            o_ref[:, col:col + LANES] = y_ref[c, pl.ds(r, rows // fold, stride=fold), :].astype(o_ref.dtype)


def _project_folded(x, w, col_scale, fold, tm=1024):
    m, k = x.shape
    n = w.shape[1]
    assert tm % (fold * BF16_SUBLANES) == 0 and m % tm == 0
    return pl.pallas_call(
        functools.partial(_proj_folded_kernel, fold=fold),
        grid=(m // tm,),
        in_specs=[pl.BlockSpec((tm, k), lambda i: (i, 0)),
                  pl.BlockSpec((k, n), lambda i: (0, 0)),
                  pl.BlockSpec((1, n), lambda i: (0, 0))],
        out_specs=pl.BlockSpec((tm // fold, fold * n), lambda i: (i, 0)),
        out_shape=jax.ShapeDtypeStruct((m // fold, fold * n), BF16),
        scratch_shapes=[pltpu.VMEM((n // LANES, tm, LANES), F32)],
        compiler_params=_cparams(("parallel",)),
    )(x, w, col_scale)


def _diff_attn_kernel(lam_ref, subg_ref, q_ref, k_ref, vt_ref, bt_ref, *rest, qb, kb, nk, lambda_init, aliased):
    o_ref, wq_ref, s_ref, acc_ref, m_ref = rest[1:] if aliased else rest
    qi = pl.program_id(2)
    q = q_ref[...]
    lane = lax.broadcasted_iota(jnp.int32, q.shape, 1)
    zero = jnp.zeros_like(q)
    wq_ref[...] = jnp.concatenate([jnp.where(lane < A_HEAD_DIM, q, zero),
                                   jnp.where(lane >= A_HEAD_DIM, q, zero)], axis=0)
    m_ref[...] = jnp.full(m_ref.shape, NEG, F32)
    acc_ref[...] = jnp.zeros(acc_ref.shape, F32)
    tpc = kb // BIAS_TILE
    qt = qb // BIAS_TILE
    reps = 2 * qt
    span = BIAS_TILE_SPAN
    q_tile0 = qi * qt

    jb0 = jnp.maximum((q_tile0 - (span - 1)) // tpc, 0)
    jb1 = jnp.minimum((q_tile0 + qt - 1 + span - 1) // tpc + 1, nk)
    n_true_band = jb1 - jb0
    far_pairs = (nk - n_true_band) // 2
    n_band = nk - 2 * far_pairs
    leftover = jnp.where(jb1 < nk, nk - 1, jb0 - 1)
    off_left = jnp.concatenate([bt_ref[0][0:1, :]] * reps, axis=1)
    off_right = jnp.concatenate([bt_ref[2 * span][0:1, :]] * reps, axis=1)
    off_zero = jnp.zeros_like(off_left)

    def band_chunk(v):
        return jnp.where(v < n_true_band, jb0 + v, leftover)

    def far_chunk(w):
        return jnp.where(w < jb0, w, w - jb0 + jb1)

    def far_off(w):
        return jnp.where(w < jb0, off_left, off_right)

    def logits(kc):
        k0 = pl.multiple_of(kc * kb, kb)
        return lax.dot_general(k_ref[pl.ds(k0, kb), :], wq_ref[...], (((1,), (1,)), ((), ())),
                               preferred_element_type=F32)

    def scores_band(kc, slot):
        s = logits(kc)
        mloc = None
        for a in range(tpc):
            tiles = [bt_ref[jnp.clip(kc * tpc + a - (q_tile0 + b), -span, span) + span] for b in range(qt)]
            rows = slice(a * BIAS_TILE, (a + 1) * BIAS_TILE)
            sa = s[rows, :] + jnp.concatenate(tiles * 2, axis=1)
            s_ref[slot, rows, :] = sa
            ma = jnp.max(sa, axis=0, keepdims=True)
            mloc = ma if mloc is None else jnp.maximum(mloc, ma)
        return mloc

    def scores_far(kc, slot, off):
        s = logits(kc)
        s_ref[slot] = s
        return jnp.max(s, axis=0, keepdims=True) + off

    def update(kc, slot, mloc, off):
        m_old = m_ref[...]
        m_new = jnp.maximum(m_old, mloc)
        alpha = jnp.exp2(m_old - m_new)
        p = jnp.exp2(s_ref[slot] - (m_new - off))
        pv = jnp.dot(vt_ref[kc], p.astype(BF16), preferred_element_type=F32)
        acc_ref[...] = acc_ref[...] * alpha + pv
        m_ref[...] = m_new

    def band_pair(j, carry):
        mloc_p, kc_p = carry
        k1, k2 = band_chunk(2 * j + 1), band_chunk(2 * j + 2)
        m1 = scores_band(k1, 1)
        update(kc_p, 0, mloc_p, off_zero)
        m2 = scores_band(k2, 0)
        update(k1, 1, m1, off_zero)
        return m2, k2

    def far_pair(j, carry):
        mloc_p, kc_p, off_p = carry
        k1, k2 = far_chunk(2 * j), far_chunk(2 * j + 1)
        o1, o2 = far_off(2 * j), far_off(2 * j + 1)
        m1 = scores_far(k1, 1, o1)
        update(kc_p, 0, mloc_p, off_p)
        m2 = scores_far(k2, 0, o2)
        update(k1, 1, m1, o1)
        return m2, k2, o2

    kc0 = band_chunk(0)
    mloc_p, kc_p = lax.fori_loop(0, n_band // 2 - 1, band_pair, (scores_band(kc0, 0), kc0))
    mloc_p, kc_p, off_p = lax.fori_loop(0, far_pairs, far_pair, (mloc_p, kc_p, off_zero))
    kc_t = band_chunk(n_band - 1)
    mloc_t = scores_band(kc_t, 1)
    update(kc_p, 0, mloc_p, off_p)
    update(kc_t, 1, mloc_t, off_zero)

    lamv = lam_ref[...]
    lam = (jnp.exp(jnp.sum(lamv[0:1] * lamv[1:2], axis=1, keepdims=True))
           - jnp.exp(jnp.sum(lamv[2:3] * lamv[3:4], axis=1, keepdims=True)) + lambda_init)
    acc = acc_ref[...]
    on = acc[:A_V_DIM] / acc[A_V_DIM:A_V_DIM + 1]
    o = on[:, :qb] - lam * on[:, qb:]
    o = o * lax.rsqrt(jnp.mean(o * o, axis=0, keepdims=True) + LN_EPS)
    o = o * (subg_ref[...] * (1.0 - lambda_init))
    o_ref[...] = o.T.astype(o_ref.dtype)


def _diff_attention(qkv, row0, b, s, prev_out, bias_tiles, lam_vecs, subg_col, lambda_init, qb=256, kb=1024):
    t = qkv.shape[0]
    kb = min(kb, s // 2)
    nk = s // kb
    assert qb % BIAS_TILE == 0 and kb % BIAS_TILE == 0 and nk % 2 == 0 and nk * kb == s
    assert row0 % s == 0 and s % qb == 0
    v = qkv[row0:row0 + b * s, 2 * A_QK_WIDTH:].reshape(b, nk, kb, A_HEADS, A_V_DIM).transpose(0, 3, 1, 4, 2)
    ones = jnp.ones((b, A_HEADS, nk, 1, kb), BF16)
    pad = jnp.zeros((b, A_HEADS, nk, VT_ROWS - A_V_DIM - 1, kb), BF16)
    vt = jnp.concatenate([v, ones, pad], axis=3)
    aliased = prev_out is not None
    kern = functools.partial(_diff_attn_kernel, qb=qb, kb=kb, nk=nk, lambda_init=lambda_init, aliased=aliased)
    n_tiles = 2 * BIAS_TILE_SPAN + 1
    q_blk0, k_blk0 = row0 // qb, row0 // s
    q_blocks = s // qb
    in_specs = [pl.BlockSpec((4, A_HEAD_DIM), lambda bi, h, qi: (0, 0)),
                pl.BlockSpec((A_V_DIM, 1), lambda bi, h, qi: (0, 0)),
                pl.BlockSpec((qb, LANES), lambda bi, h, qi: (q_blk0 + bi * q_blocks + qi, h)),
                pl.BlockSpec((s, LANES), lambda bi, h, qi: (k_blk0 + bi, A_HEADS + h)),
                pl.BlockSpec((None, None, nk, VT_ROWS, kb), lambda bi, h, qi: (bi, h, 0, 0, 0)),
                pl.BlockSpec((None, n_tiles, BIAS_TILE, BIAS_TILE), lambda bi, h, qi: (h, 0, 0, 0))]
    args = [lam_vecs, subg_col, qkv, qkv, vt, bias_tiles]
    if aliased:
        in_specs.append(pl.BlockSpec(memory_space=pl.ANY))
        args.append(prev_out)
    return pl.pallas_call(
        kern,
        grid=(b, A_HEADS, q_blocks),
        in_specs=in_specs,
        out_specs=pl.BlockSpec((qb, LANES), lambda bi, h, qi: (q_blk0 + bi * q_blocks + qi, h)),
        out_shape=jax.ShapeDtypeStruct((t, A_HEADS * A_V_DIM), BF16),
        scratch_shapes=[pltpu.VMEM((2 * qb, LANES), BF16),
                        pltpu.VMEM((2, kb, 2 * qb), F32),
                        pltpu.VMEM((VT_ROWS, 2 * qb), F32),
                        pltpu.VMEM((1, 2 * qb), F32)],
        input_output_aliases={len(args) - 1: 0} if aliased else {},
        compiler_params=_cparams(("parallel", "parallel", "arbitrary")),
    )(*args)


def _diff_bias_tiles(a_cols):
    t = BIAS_TILE
    e = (jnp.arange(2 * BIAS_TILE_SPAN + 1, dtype=jnp.int32) - BIAS_TILE_SPAN) * t
    j = jnp.arange(t, dtype=jnp.int32)
    rel = e[:, None, None] + j[None, :, None] - j[None, None, :]
    return _bias_lookup(a_cols, rel) * LOG2E


def _dilated_kernel(q_ref, kp_ref, kc_ref, kn_ref, vp_ref, vc_ref, vn_ref, bias_ref, *rest,
                    lb, sb, radius, n_blk, aliased):
    o_ref, lse_ref = rest[2:] if aliased else rest
    li = pl.program_id(2)
    kwin = jnp.concatenate([kp_ref[lb - radius:, :], kc_ref[...], kn_ref[:radius, :]], axis=0)
    vwin = jnp.concatenate([vp_ref[lb - radius:, :], vc_ref[...], vn_ref[:radius, :]], axis=0)
    q = q_ref[...]
    swin = sb + 2 * radius
    jj = lax.broadcasted_iota(jnp.int32, (swin, sb), 0)
    for u in range(lb // sb):
        rows = slice(u * sb, u * sb + swin)
        valid = jnp.logical_and(jnp.logical_or(li > 0, jj + u * sb >= radius),
                                jnp.logical_or(li < n_blk - 1, jj + u * sb < lb + radius))
        outs, lses = [], []
        for h in range(B_HEADS):
            cs = slice(h * B_HEAD_DIM, (h + 1) * B_HEAD_DIM)
            s = lax.dot_general(kwin[rows, cs], q[u * sb:(u + 1) * sb, cs], (((1,), (1,)), ((), ())),
                                preferred_element_type=F32)
            s = jnp.where(valid, s + bias_ref[h], NEG)
            m = jnp.max(s, axis=0, keepdims=True)
            p = jnp.exp2(s - m)
            l = jnp.sum(p, axis=0, keepdims=True)
            o = lax.dot_general(vwin[rows, cs], p.astype(BF16), (((0,), (0,)), ((), ())),
                                preferred_element_type=F32)
            outs.append(o / l)
            lses.append(jnp.broadcast_to(m + jnp.log2(l), (B_HEAD_DIM, sb)))
        o_ref[u * sb:(u + 1) * sb, :] = jnp.concatenate(outs, axis=0).T
        lse_ref[u * sb:(u + 1) * sb, :] = jnp.concatenate(lses, axis=0).T


def _dilated_group(x, row0, b, s, prev, dilation, radius, cols):
    t = x.shape[0] * dilation
    n_col = 3
    assert x.shape[1] == dilation * n_col * B_GROUP_WIDTH
    sub_len = s // dilation
    lb = min(512, sub_len)
    sb = lb
    n_blk = sub_len // lb
    win = sb + 2 * radius
    assert row0 % (dilation * lb) == 0 and n_blk * lb == sub_len and lb % sb == 0
    blk0 = row0 // dilation // lb
    i = jnp.arange(sb, dtype=jnp.int32)[None, :]
    j = jnp.arange(win, dtype=jnp.int32)[:, None]
    rel_sub = j - radius - i
    bias = _bias_lookup(cols, rel_sub * dilation) * LOG2E
    bias = jnp.where((jnp.abs(rel_sub) <= radius)[None], bias, NEG)

    def spec(col, shift):
        def index(bi, r, li):
            return (blk0 + bi * n_blk + jnp.clip(li + shift, 0, n_blk - 1), r * n_col + col)
        return pl.BlockSpec((lb, B_GROUP_WIDTH), index)

    aliased = prev is not None
    kern = functools.partial(_dilated_kernel, lb=lb, sb=sb, radius=radius, n_blk=n_blk, aliased=aliased)
    out_spec = pl.BlockSpec((lb, B_GROUP_WIDTH), lambda bi, r, li: (blk0 + bi * n_blk + li, r))
    shape = jax.ShapeDtypeStruct((t // dilation, dilation * B_GROUP_WIDTH), F32)
    in_specs = [spec(0, 0),
                spec(1, -1), spec(1, 0), spec(1, 1),
                spec(2, -1), spec(2, 0), spec(2, 1),
                pl.BlockSpec((B_HEADS, win, sb), lambda bi, r, li: (0, 0, 0))]
    args = [x, x, x, x, x, x, x, bias]
    aliases = {}
    if aliased:
        in_specs += [pl.BlockSpec(memory_space=pl.ANY)] * 2
        args += list(prev)
        aliases = {len(args) - 2: 0, len(args) - 1: 1}
    return pl.pallas_call(
        kern,
        grid=(b, dilation, n_blk),
        in_specs=in_specs,
        out_specs=[out_spec, out_spec],
        out_shape=[shape, shape],
        input_output_aliases=aliases,
        compiler_params=_cparams(("parallel", "parallel", "arbitrary")),
    )(*args)


def _layer_norm(z, g, b):
    mu = jnp.mean(z, axis=-1, keepdims=True)
    zc = z - mu
    var = jnp.mean(zc * zc, axis=-1, keepdims=True)
    return zc * lax.rsqrt(var + LN_EPS) * g + b


def _route(logits):
    lane = lax.broadcasted_iota(jnp.int32, logits.shape, 1).astype(F32)
    far = float(LANES)
    is_group = lane < N_GROUPS
    gl = jnp.where(is_group, logits, NEG)
    gmax = jnp.max(gl, axis=1, keepdims=True)
    gidx = jnp.min(jnp.where(gl == gmax, lane, far), axis=1, keepdims=True)
    gden = jnp.sum(jnp.where(is_group, jnp.exp(gl - gmax), 0.0), axis=1, keepdims=True)
    g_w = 1.0 / gden
    lo = N_GROUPS + EXPERTS_PER_GROUP * gidx
    in_group = jnp.logical_and(lane >= lo, lane < lo + EXPERTS_PER_GROUP)
    el = jnp.where(in_group, logits, NEG)
    v1 = jnp.max(el, axis=1, keepdims=True)
    i1 = jnp.min(jnp.where(el == v1, lane, far), axis=1, keepdims=True)
    el2 = jnp.where(lane == i1, NEG, el)
    v2 = jnp.max(el2, axis=1, keepdims=True)
    i2 = jnp.min(jnp.where(el2 == v2, lane, far), axis=1, keepdims=True)
    e2 = jnp.exp(v2 - v1)
    den = 1.0 + e2
    gate0 = g_w / den
    gate1 = g_w * e2 / den
    out = jnp.where(lane == 0.0, gate0, 0.0)
    out = jnp.where(lane == 1.0, gate1, out)
    out = jnp.where(lane == 2.0, i1 - N_GROUPS, out)
    out = jnp.where(lane == 3.0, i2 - N_GROUPS, out)
    return out


def _post_mixer(x_ref, y, w_ref, g_ref, b_ref, wr_ref, x1_ref, route_ref):
    h = jnp.dot(y, w_ref[...], preferred_element_type=F32)
    x1 = _layer_norm(ALPHA * x_ref[...] + h, g_ref[...], b_ref[...])
    x1_ref[...] = x1
    route_ref[...] = _route(jnp.dot(x1.astype(BF16), wr_ref[...], preferred_element_type=F32))


def _post_diff_kernel(x_ref, o_ref, w_ref, g_ref, b_ref, wr_ref, x1_ref, route_ref):
    _post_mixer(x_ref, o_ref[...], w_ref, g_ref, b_ref, wr_ref, x1_ref, route_ref)


def _unfold(src_ref, scratch_ref, fold):
    if fold == 1:
        return src_ref[...]
    n_tiles, rows, _ = scratch_ref.shape
    for r in range(fold):
        for c in range(n_tiles):
            col = (r * n_tiles + c) * LANES
            scratch_ref[c, pl.ds(r, rows // fold, stride=fold), :] = src_ref[:, col:col + LANES]
    return jnp.concatenate([scratch_ref[c] for c in range(n_tiles)], axis=1)


def _post_dilated_kernel(x_ref, o0_ref, o1_ref, o2_ref, l0_ref, l1_ref, l2_ref, w_ref, g_ref, b_ref, wr_ref,
                         x1_ref, route_ref, *scratch):
    folds = [dil for (_, dil) in B_CONFIGS]
    o_refs, l_refs = (o0_ref, o1_ref, o2_ref), (l0_ref, l1_ref, l2_ref)
    o0, o1, o2 = [_unfold(o_refs[g], scratch[g], folds[g]) for g in range(B_GROUPS)]
    l0, l1, l2 = [_unfold(l_refs[g], scratch[B_GROUPS + g], folds[g]) for g in range(B_GROUPS)]
    m = jnp.maximum(jnp.maximum(l0, l1), l2)
    e0, e1, e2 = jnp.exp2(l0 - m), jnp.exp2(l1 - m), jnp.exp2(l2 - m)
    den = e0 + e1 + e2
    o = (o0 * (e0 / den) + o1 * (e1 / den)) + o2 * (e2 / den)
    _post_mixer(x_ref, o.astype(BF16), w_ref, g_ref, b_ref, wr_ref, x1_ref, route_ref)


def _post_mixer_call(kern, x, mixer_outs, w_out, gain, bias, w_route, scratch_widths=(), tm=512):
    t, d = x.shape
    row = lambda width: pl.BlockSpec((tm, width), lambda i: (i, 0))
    folded = lambda a: pl.BlockSpec((tm * a.shape[0] // t, a.shape[1]), lambda i: (i, 0))
    whole = lambda a: pl.BlockSpec(a.shape, lambda i: (0,) * a.ndim)
    return pl.pallas_call(
        kern,
        grid=(t // tm,),
        in_specs=[row(d)] + [folded(a) for a in mixer_outs]
        + [whole(w_out), whole(gain), whole(bias), whole(w_route)],
        out_specs=[row(d), row(LANES)],
        out_shape=[jax.ShapeDtypeStruct((t, d), F32), jax.ShapeDtypeStruct((t, LANES), F32)],
        scratch_shapes=[pltpu.VMEM((width // LANES, tm, LANES), F32) for width in scratch_widths],
        compiler_params=_cparams(("parallel",)),
    )(x, *mixer_outs, w_out, gain, bias, w_route)


def _moe_mlp_kernel(blk_ref, used_ref, src_ref, x_hbm, gw_ref, w1_ref, w3_ref, w2_ref, y_ref,
                    w1b_ref, w3b_ref, w2b_ref, xbuf0_ref, xbuf1_ref, sem_ref):
    i = pl.program_id(0)
    n_used = used_ref[0]
    active = i < n_used
    parity = lax.rem(i, 2)
    bufs = (xbuf0_ref, xbuf1_ref)
    new_expert = jnp.logical_or(i == 0, blk_ref[i] != blk_ref[jnp.maximum(i - 1, 0)])

    def row_copy(block, r, slot):
        tok = src_ref[block * MOE_ROWS + r]
        return pltpu.make_async_copy(x_hbm.at[pl.ds(tok, 1), :], bufs[slot].at[pl.ds(r, 1), :], sem_ref.at[slot])

    def start_rows(block, slot):
        for r in range(MOE_ROWS):
            row_copy(block, r, slot).start()

    def wait_rows(block, slot):
        for r in range(MOE_ROWS):
            row_copy(block, r, slot).wait()

    @pl.when(i == 0)
    def _():
        start_rows(0, 0)

    @pl.when(jnp.logical_and(active, new_expert))
    def _():
        w1b_ref[...] = w1_ref[...].astype(BF16)
        w3b_ref[...] = w3_ref[...].astype(BF16)
        w2b_ref[...] = w2_ref[...].astype(BF16)

    def compute_block(slot):
        wait_rows(i, slot)
        start_rows(i + 1, 1 - slot)
        x = bufs[slot][...].astype(BF16)
        h1 = jnp.dot(x, w1b_ref[...], preferred_element_type=F32)
        h3 = jnp.dot(x, w3b_ref[...], preferred_element_type=F32)
        h = (h1 * jax.nn.sigmoid(h1)) * h3
        y = jnp.dot(h.astype(BF16), w2b_ref[...], preferred_element_type=F32)
        y_ref[...] = (y * gw_ref[...]).astype(y_ref.dtype)

    for slot in range(2):
        pl.when(jnp.logical_and(active, parity == slot))(functools.partial(compute_block, slot))
        pl.when(jnp.logical_and(i == n_used, parity == slot))(functools.partial(wait_rows, i, slot))

    @pl.when(jnp.logical_not(active))
    def _():
        y_ref[...] = jnp.zeros(y_ref.shape, y_ref.dtype)


def _moe_mlp(x, src_tok, gw, blk_exp, n_used, layer, w1, w3, w2):
    p = src_tok.shape[0]
    d = x.shape[1]
    n_blk = p // MOE_ROWS
    grid_spec = pltpu.PrefetchScalarGridSpec(
        num_scalar_prefetch=3,
        grid=(n_blk,),
        in_specs=[pl.BlockSpec(memory_space=pl.ANY),
                  pl.BlockSpec((MOE_ROWS, 1), lambda i, be, nu, st: (i, 0)),
                  pl.BlockSpec((None, None, d, D_EXPERT), lambda i, be, nu, st: (layer, be[i], 0, 0)),
                  pl.BlockSpec((None, None, d, D_EXPERT), lambda i, be, nu, st: (layer, be[i], 0, 0)),
                  pl.BlockSpec((None, None, D_EXPERT, d), lambda i, be, nu, st: (layer, be[i], 0, 0))],
        out_specs=pl.BlockSpec((MOE_ROWS, d), lambda i, be, nu, st: (i, 0)),
        scratch_shapes=[pltpu.VMEM((d, D_EXPERT), BF16), pltpu.VMEM((d, D_EXPERT), BF16),
                        pltpu.VMEM((D_EXPERT, d), BF16),
                        pltpu.VMEM((MOE_ROWS, d), F32), pltpu.VMEM((MOE_ROWS, d), F32),
                        pltpu.SemaphoreType.DMA((2,))],
    )
    return pl.pallas_call(
        _moe_mlp_kernel,
        grid_spec=grid_spec,
        out_shape=jax.ShapeDtypeStruct((p, d), BF16),
        compiler_params=_cparams(("arbitrary",)),
    )(blk_exp, n_used, src_tok, x, gw, w1, w3, w2)


def _combine_ln_kernel(x_ref, ya_ref, yb_ref, g_ref, b_ref, *rest):
    o_ref = rest[-1]
    z = ALPHA * x_ref[...] + (ya_ref[...].astype(F32) + yb_ref[...].astype(F32))
    o_ref[...] = _layer_norm(z, g_ref[...], b_ref[...])


def _combine_ln(x1, row0, ya, yb, gain, bias, prev_out, tm=1024):
    t, d = x1.shape
    n = ya.shape[0]
    assert row0 % tm == 0 and n % tm == 0
    blk0 = row0 // tm
    row_x = pl.BlockSpec((tm, d), lambda i: (blk0 + i, 0))
    row_y = pl.BlockSpec((tm, d), lambda i: (i, 0))
    vec = pl.BlockSpec((1, d), lambda i: (0, 0))
    in_specs = [row_x, row_y, row_y, vec, vec]
    args = [x1, ya, yb, gain, bias]
    aliases = {}
    if prev_out is not None:
        in_specs.append(pl.BlockSpec(memory_space=pl.ANY))
        args.append(prev_out)
        aliases = {len(args) - 1: 0}
    return pl.pallas_call(
        _combine_ln_kernel,
        grid=(n // tm,),
        in_specs=in_specs,
        out_specs=row_x,
        out_shape=jax.ShapeDtypeStruct((t, d), F32),
        input_output_aliases=aliases,
        compiler_params=_cparams(("parallel",)),
    )(*args)


def _moe(x1, route, row0, n_tok, prev_out, layer, w1, w3, w2, gain, bias):
    t = n_tok
    n_assign = 2 * t
    take = lambda a, idx: a.at[idx].get(mode='promise_in_bounds')
    route = lax.slice_in_dim(route, row0, row0 + t, axis=0)
    gate = route[:, 0:2].reshape(n_assign)
    e = route[:, 2:4].astype(jnp.int32).reshape(n_assign)
    order = jnp.argsort(e).astype(jnp.int32)
    rank = jnp.argsort(order).astype(jnp.int32)
    experts = jnp.arange(N_EXPERTS, dtype=jnp.int32)
    counts = jnp.sum((e[:, None] == experts[None, :]).astype(jnp.int32), axis=0)
    padded = (counts + MOE_ROWS - 1) // MOE_ROWS * MOE_ROWS
    starts = jnp.cumsum(counts) - counts
    pends = jnp.cumsum(padded)
    pstarts = pends - padded
    pos = (take(pstarts - starts, e) + rank).reshape(t, 2)
    n_rows = n_assign + N_EXPERTS * MOE_ROWS
    n_blk = n_rows // MOE_ROWS
    blk_row0 = jnp.arange(n_blk, dtype=jnp.int32) * MOE_ROWS
    blk_exp = jnp.minimum(jnp.sum((pends[None, :] <= blk_row0[:, None]).astype(jnp.int32), axis=1), N_EXPERTS - 1)
    n_used = (pends[N_EXPERTS - 1] // MOE_ROWS).astype(jnp.int32)[None]
    base = jnp.repeat(take(starts - pstarts, blk_exp), MOE_ROWS)
    limit = jnp.repeat(take(starts + counts, blk_exp), MOE_ROWS)
    sidx = jnp.arange(n_rows, dtype=jnp.int32) + base
    valid = sidx < limit
    a_src = take(order, jnp.clip(sidx, 0, n_assign - 1))
    src_tok = jnp.where(valid, a_src // 2, 0) + row0
    gw = jnp.where(valid, take(gate, a_src), 0.0)
    y = _moe_mlp(x1, src_tok, gw[:, None], blk_exp, n_used, layer, w1, w3, w2)
    ya = take(y, pos[:, 0])
    yb = take(y, pos[:, 1])
    return _combine_ln(x1, row0, ya, yb, gain, bias, prev_out)


def _col_scale(width, n_scaled, scale):
    return jnp.where(jnp.arange(width) < n_scaled, scale, 1.0).astype(F32)[None, :]


def _trunk(x, batch_shapes, rel_bias_table, a_w_in, a_w_out, a_lambda_q1, a_lambda_k1, a_lambda_q2, a_lambda_k2,
           a_subln_g, b_w_in, b_w_out, ln_gain, ln_bias, router_group, router_expert,
           expert_w1, expert_w3, expert_w2):
    a_cols = rel_bias_table[:, :A_HEADS].T
    bias_tiles = _diff_bias_tiles(a_cols)
    a_scale = _col_scale(3 * A_QK_WIDTH, A_QK_WIDTH, A_HEAD_DIM ** -0.5 * LOG2E)
    b_scale = _col_scale(3 * B_GROUP_WIDTH, B_GROUP_WIDTH, B_HEAD_DIM ** -0.5 * LOG2E)
    offsets = []
    start = 0
    for (b, s) in batch_shapes:
        offsets.append((start, b, s))
        start += b * s
    t = x.shape[0]

    for i in range(DEPTH):
        li = i // N_MIXERS
        w_route = jnp.concatenate(
            [router_group[i], router_expert[i],
             jnp.zeros((D_MODEL, LANES - N_GROUPS - N_EXPERTS), F32)], axis=1).astype(BF16)
        gain1, bias1 = ln_gain[i, 0][None, :], ln_bias[i, 0][None, :]
        gain2, bias2 = ln_gain[i, 1][None, :], ln_bias[i, 1][None, :]
        if i % N_MIXERS == 0:
            lambda_init = 0.8 - 0.6 * math.exp(-0.3 * i)
            qkv = _project(x, a_w_in[li].astype(BF16), a_scale)
            lam_vecs = jnp.stack([a_lambda_q1[li], a_lambda_k1[li], a_lambda_q2[li], a_lambda_k2[li]]).astype(F32)
            subg_col = a_subln_g[li].astype(F32)[:, None]
            o = None
            for (st, b, s) in offsets:
                o = _diff_attention(qkv, st, b, s, o, bias_tiles, lam_vecs, subg_col, lambda_init)
            mixer_outs = [o]
            kern = _post_diff_kernel
            w_out = a_w_out[li].astype(BF16)
            scratch_widths = ()
        else:
            outs, lses = [], []
            for g, (window, dil) in enumerate(B_CONFIGS):
                c0 = A_HEADS + g * B_HEADS
                cols = rel_bias_table[:, c0:c0 + B_HEADS].T
                w_g = jnp.concatenate([b_w_in[li][:, (sec * B_GROUPS + g) * B_GROUP_WIDTH:
                                                  (sec * B_GROUPS + g + 1) * B_GROUP_WIDTH] for sec in range(3)],
                                      axis=1).astype(BF16)
                qkv_g = _project_folded(x, w_g, b_scale, dil)
                pair = None
                for (st, b, s) in offsets:
                    pair = _dilated_group(qkv_g, st, b, s, pair, dil, window // (2 * dil), cols)
                outs.append(pair[0])
                lses.append(pair[1])
            mixer_outs = outs + lses
            kern = _post_dilated_kernel
            w_out = b_w_out[li].astype(BF16)
            scratch_widths = (B_GROUP_WIDTH,) * (2 * B_GROUPS)
        x1, route = _post_mixer_call(kern, x, mixer_outs, w_out, gain1, bias1, w_route, scratch_widths)
        x = _moe(x1, route, 0, t, None, i, expert_w1, expert_w3, expert_w2, gain2, bias2)
    return x


def kernel(x_prompt, x_sample, rel_bias_table, a_w_in, a_w_out, a_lambda_q1, a_lambda_k1, a_lambda_q2, a_lambda_k2, a_subln_g, b_w_in, b_w_out, ln_gain, ln_bias, router_group, router_expert, expert_w1, expert_w3, expert_w2):
    shapes = [x_prompt.shape[:2], x_sample.shape[:2]]
    x = jnp.concatenate([x_prompt.reshape(-1, D_MODEL), x_sample.reshape(-1, D_MODEL)], axis=0)
    y = _trunk(x, shapes, rel_bias_table, a_w_in, a_w_out, a_lambda_q1, a_lambda_k1, a_lambda_q2, a_lambda_k2,
               a_subln_g, b_w_in, b_w_out, ln_gain, ln_bias, router_group, router_expert,
               expert_w1, expert_w3, expert_w2)
    n_prompt = x_prompt.shape[0] * x_prompt.shape[1]
    return (y[:n_prompt].reshape(x_prompt.shape), y[n_prompt:].reshape(x_sample.shape))
```

```python
import functools
import math

import jax
import jax.numpy as jnp
from jax import lax
from jax.experimental import pallas as pl
from jax.experimental.pallas import tpu as pltpu

F32 = jnp.float32
BF16 = jnp.bfloat16

D_MODEL = 1024
DEPTH = 4
N_MIXERS = 2

A_HEADS = 8
A_HEAD_DIM = 64
A_QK_WIDTH = A_HEADS * 2 * A_HEAD_DIM
A_V_DIM = 2 * A_HEAD_DIM

B_CONFIGS = ((128, 1), (512, 4), (2048, 16))
B_GROUPS = 3
B_HEADS = 8
B_HEAD_DIM = 64
B_GROUP_WIDTH = B_HEADS * B_HEAD_DIM
B_PROJ_WIDTH = B_GROUPS * B_GROUP_WIDTH

NUM_BUCKETS = 32
MAX_DISTANCE = 1024

N_GROUPS = 4
EXPERTS_PER_GROUP = 8
N_EXPERTS = N_GROUPS * EXPERTS_PER_GROUP
D_EXPERT = 512

ALPHA = (2.0 * DEPTH) ** 0.25
LN_EPS = 1e-5
NEG = -1e30
LOG2E = math.log2(math.e)

LANES = 128
BF16_SUBLANES = 16
MOE_ROWS = 256
MOE_ROW_BUFFERS = 3
BIAS_TILE = 256
BIAS_TILE_SPAN = 4
VT_ROWS = A_V_DIM + BF16_SUBLANES
VMEM_LIMIT = 56 * 1024 * 1024


def _cparams(sem):
    return pltpu.CompilerParams(dimension_semantics=sem, vmem_limit_bytes=VMEM_LIMIT)


def _rel_bucket(rel):
    nb = NUM_BUCKETS // 2
    max_exact = nb // 2
    n = jnp.abs(rel)
    side = jnp.where(rel > 0, nb, 0)
    nf = jnp.maximum(n, 1).astype(F32)
    large = max_exact + (jnp.log(nf / max_exact) / math.log(MAX_DISTANCE / max_exact)
                         * (nb - max_exact)).astype(jnp.int32)
    large = jnp.minimum(large, nb - 1)
    return side + jnp.where(n < max_exact, n, large)


def _bias_lookup(cols, rel):
    bucket = _rel_bucket(rel)[None]
    shape = (cols.shape[0],) + (1,) * rel.ndim
    out = jnp.zeros((cols.shape[0],) + rel.shape, F32)
    for b in range(NUM_BUCKETS):
        out = jnp.where(bucket == b, cols[:, b].astype(F32).reshape(shape), out)
    return out


def _proj_kernel(x_ref, w_ref, s_ref, o_ref):
    acc = jnp.dot(x_ref[...].astype(BF16), w_ref[...], preferred_element_type=F32)
    o_ref[...] = (acc * s_ref[...]).astype(o_ref.dtype)


def _project(x, w, col_scale, tm=1024, tn=512):
    m, k = x.shape
    n = w.shape[1]
    return pl.pallas_call(
        _proj_kernel,
        grid=(m // tm, n // tn),
        in_specs=[pl.BlockSpec((tm, k), lambda i, j: (i, 0)),
                  pl.BlockSpec((k, tn), lambda i, j: (0, j)),
                  pl.BlockSpec((1, tn), lambda i, j: (0, j))],
        out_specs=pl.BlockSpec((tm, tn), lambda i, j: (i, j)),
        out_shape=jax.ShapeDtypeStruct((m, n), BF16),
        compiler_params=_cparams(("parallel", "arbitrary")),
    )(x, w, col_scale)


def _proj_folded_kernel(x_ref, w_ref, s_ref, o_ref, y_ref, *, fold):
    y = jnp.dot(x_ref[...].astype(BF16), w_ref[...], preferred_element_type=F32) * s_ref[...]
    if fold == 1:
        o_ref[...] = y.astype(o_ref.dtype)
        return
    n_tiles, rows, _ = y_ref.shape
    for c in range(n_tiles):
        y_ref[c] = y[:, c * LANES:(c + 1) * LANES]
    for r in range(fold):
        for c in range(n_tiles):
            col = (r * n_tiles + c) * LANES
            o_ref[:, col:col + LANES] = y_ref[c, pl.ds(r, rows // fold, stride=fold), :].astype(o_ref.dtype)


def _project_folded(x, w, col_scale, fold, tm=1024):
    m, k = x.shape
    n = w.shape[1]
    assert tm % (fold * BF16_SUBLANES) == 0 and m % tm == 0
    return pl.pallas_call(
        functools.partial(_proj_folded_kernel, fold=fold),
        grid=(m // tm,),
        in_specs=[pl.BlockSpec((tm, k), lambda i: (i, 0)),
                  pl.BlockSpec((k, n), lambda i: (0, 0)),
                  pl.BlockSpec((1, n), lambda i: (0, 0))],
        out_specs=pl.BlockSpec((tm // fold, fold * n), lambda i: (i, 0)),
        out_shape=jax.ShapeDtypeStruct((m // fold, fold * n), BF16),
        scratch_shapes=[pltpu.VMEM((n // LANES, tm, LANES), F32)],
        compiler_params=_cparams(("parallel",)),
    )(x, w, col_scale)


def _diff_attn_kernel(lam_ref, subg_ref, q_ref, k_ref, vt_ref, bt_ref, *rest, qb, kb, nk, lambda_init, aliased):
    o_ref, wq_ref, s_ref, acc_ref, m_ref = rest[1:] if aliased else rest
    qi = pl.program_id(2)
    q = q_ref[...]
    lane = lax.broadcasted_iota(jnp.int32, q.shape, 1)
    zero = jnp.zeros_like(q)
    wq_ref[...] = jnp.concatenate([jnp.where(lane < A_HEAD_DIM, q, zero),
                                   jnp.where(lane >= A_HEAD_DIM, q, zero)], axis=0)
    m_ref[...] = jnp.full(m_ref.shape, NEG, F32)
    acc_ref[...] = jnp.zeros(acc_ref.shape, F32)
    tpc = kb // BIAS_TILE
    qt = qb // BIAS_TILE
    reps = 2 * qt
    span = BIAS_TILE_SPAN
    q_tile0 = qi * qt

    jb0 = jnp.maximum((q_tile0 - (span - 1)) // tpc, 0)
    jb1 = jnp.minimum((q_tile0 + qt - 1 + span - 1) // tpc + 1, nk)
    n_true_band = jb1 - jb0
    far_pairs = (nk - n_true_band) // 2
    n_band = nk - 2 * far_pairs
    leftover = jnp.where(jb1 < nk, nk - 1, jb0 - 1)
    off_left = jnp.concatenate([bt_ref[0][0:1, :]] * reps, axis=1)
    off_right = jnp.concatenate([bt_ref[2 * span][0:1, :]] * reps, axis=1)
    off_zero = jnp.zeros_like(off_left)

    def band_chunk(v):
        return jnp.where(v < n_true_band, jb0 + v, leftover)

    def far_chunk(w):
        return jnp.where(w < jb0, w, w - jb0 + jb1)

    def far_off(w):
        return jnp.where(w < jb0, off_left, off_right)

    def logits(kc):
        k0 = pl.multiple_of(kc * kb, kb)
        return lax.dot_general(k_ref[pl.ds(k0, kb), :], wq_ref[...], (((1,), (1,)), ((), ())),
                               preferred_element_type=F32)

    def scores_band(kc, slot):
        s = logits(kc)
        mloc = None
        for a in range(tpc):
            tiles = [bt_ref[jnp.clip(kc * tpc + a - (q_tile0 + b), -span, span) + span] for b in range(qt)]
            rows = slice(a * BIAS_TILE, (a + 1) * BIAS_TILE)
            sa = s[rows, :] + jnp.concatenate(tiles * 2, axis=1)
            s_ref[slot, rows, :] = sa
            ma = jnp.max(sa, axis=0, keepdims=True)
            mloc = ma if mloc is None else jnp.maximum(mloc, ma)
        return mloc

    def scores_far(kc, slot, off):
        s = logits(kc)
        s_ref[slot] = s
        return jnp.max(s, axis=0, keepdims=True) + off

    def update(kc, slot, mloc, off):
        m_old = m_ref[...]
        m_new = jnp.maximum(m_old, mloc)
        alpha = jnp.exp2(m_old - m_new)
        p = jnp.exp2(s_ref[slot] - (m_new - off))
        pv = jnp.dot(vt_ref[kc], p.astype(BF16), preferred_element_type=F32)
        acc_ref[...] = acc_ref[...] * alpha + pv
        m_ref[...] = m_new

    def band_pair(j, carry):
        mloc_p, kc_p = carry
        k1, k2 = band_chunk(2 * j + 1), band_chunk(2 * j + 2)
        m1 = scores_band(k1, 1)
        update(kc_p, 0, mloc_p, off_zero)
        m2 = scores_band(k2, 0)
        update(k1, 1, m1, off_zero)
        return m2, k2

    def far_pair(j, carry):
        mloc_p, kc_p, off_p = carry
        k1, k2 = far_chunk(2 * j), far_chunk(2 * j + 1)
        o1, o2 = far_off(2 * j), far_off(2 * j + 1)
        m1 = scores_far(k1, 1, o1)
        update(kc_p, 0, mloc_p, off_p)
        m2 = scores_far(k2, 0, o2)
        update(k1, 1, m1, o1)
        return m2, k2, o2

    kc0 = band_chunk(0)
    mloc_p, kc_p = lax.fori_loop(0, n_band // 2 - 1, band_pair, (scores_band(kc0, 0), kc0))
    mloc_p, kc_p, off_p = lax.fori_loop(0, far_pairs, far_pair, (mloc_p, kc_p, off_zero))
    kc_t = band_chunk(n_band - 1)
    mloc_t = scores_band(kc_t, 1)
    update(kc_p, 0, mloc_p, off_p)
    update(kc_t, 1, mloc_t, off_zero)

    lamv = lam_ref[...]
    lam = (jnp.exp(jnp.sum(lamv[0:1] * lamv[1:2], axis=1, keepdims=True))
           - jnp.exp(jnp.sum(lamv[2:3] * lamv[3:4], axis=1, keepdims=True)) + lambda_init)
    acc = acc_ref[...]
    on = acc[:A_V_DIM] / acc[A_V_DIM:A_V_DIM + 1]
    o = on[:, :qb] - lam * on[:, qb:]
    o = o * lax.rsqrt(jnp.mean(o * o, axis=0, keepdims=True) + LN_EPS)
    o = o * (subg_ref[...] * (1.0 - lambda_init))
    o_ref[...] = o.T.astype(o_ref.dtype)


def _diff_attention(qkv, row0, b, s, prev_out, bias_tiles, lam_vecs, subg_col, lambda_init, qb=256, kb=1024):
    t = qkv.shape[0]
    kb = min(kb, s // 2)
    nk = s // kb
    assert qb % BIAS_TILE == 0 and kb % BIAS_TILE == 0 and nk % 2 == 0 and nk * kb == s
    assert row0 % s == 0 and s % qb == 0
    v = qkv[row0:row0 + b * s, 2 * A_QK_WIDTH:].reshape(b, nk, kb, A_HEADS, A_V_DIM).transpose(0, 3, 1, 4, 2)
    ones = jnp.ones((b, A_HEADS, nk, 1, kb), BF16)
    pad = jnp.zeros((b, A_HEADS, nk, VT_ROWS - A_V_DIM - 1, kb), BF16)
    vt = jnp.concatenate([v, ones, pad], axis=3)
    aliased = prev_out is not None
    kern = functools.partial(_diff_attn_kernel, qb=qb, kb=kb, nk=nk, lambda_init=lambda_init, aliased=aliased)
    n_tiles = 2 * BIAS_TILE_SPAN + 1
    q_blk0, k_blk0 = row0 // qb, row0 // s
    q_blocks = s // qb
    in_specs = [pl.BlockSpec((4, A_HEAD_DIM), lambda bi, h, qi: (0, 0)),
                pl.BlockSpec((A_V_DIM, 1), lambda bi, h, qi: (0, 0)),
                pl.BlockSpec((qb, LANES), lambda bi, h, qi: (q_blk0 + bi * q_blocks + qi, h)),
                pl.BlockSpec((s, LANES), lambda bi, h, qi: (k_blk0 + bi, A_HEADS + h)),
                pl.BlockSpec((None, None, nk, VT_ROWS, kb), lambda bi, h, qi: (bi, h, 0, 0, 0)),
                pl.BlockSpec((None, n_tiles, BIAS_TILE, BIAS_TILE), lambda bi, h, qi: (h, 0, 0, 0))]
    args = [lam_vecs, subg_col, qkv, qkv, vt, bias_tiles]
    if aliased:
        in_specs.append(pl.BlockSpec(memory_space=pl.ANY))
        args.append(prev_out)
    return pl.pallas_call(
        kern,
        grid=(b, A_HEADS, q_blocks),
        in_specs=in_specs,
        out_specs=pl.BlockSpec((qb, LANES), lambda bi, h, qi: (q_blk0 + bi * q_blocks + qi, h)),
        out_shape=jax.ShapeDtypeStruct((t, A_HEADS * A_V_DIM), BF16),
        scratch_shapes=[pltpu.VMEM((2 * qb, LANES), BF16),
                        pltpu.VMEM((2, kb, 2 * qb), F32),
                        pltpu.VMEM((VT_ROWS, 2 * qb), F32),
                        pltpu.VMEM((1, 2 * qb), F32)],
        input_output_aliases={len(args) - 1: 0} if aliased else {},
        compiler_params=_cparams(("parallel", "parallel", "arbitrary")),
    )(*args)


def _diff_bias_tiles(a_cols):
    t = BIAS_TILE
    e = (jnp.arange(2 * BIAS_TILE_SPAN + 1, dtype=jnp.int32) - BIAS_TILE_SPAN) * t
    j = jnp.arange(t, dtype=jnp.int32)
    rel = e[:, None, None] + j[None, :, None] - j[None, None, :]
    return _bias_lookup(a_cols, rel) * LOG2E


def _dilated_kernel(q_ref, kp_ref, kc_ref, kn_ref, vp_ref, vc_ref, vn_ref, bias_ref, *rest,
                    lb, sb, radius, n_blk, aliased):
    o_ref, lse_ref = rest[2:] if aliased else rest
    li = pl.program_id(2)
    kwin = jnp.concatenate([kp_ref[lb - radius:, :], kc_ref[...], kn_ref[:radius, :]], axis=0)
    vwin = jnp.concatenate([vp_ref[lb - radius:, :], vc_ref[...], vn_ref[:radius, :]], axis=0)
    q = q_ref[...]
    swin = sb + 2 * radius
    jj = lax.broadcasted_iota(jnp.int32, (swin, sb), 0)
    for u in range(lb // sb):
        rows = slice(u * sb, u * sb + swin)
        valid = jnp.logical_and(jnp.logical_or(li > 0, jj + u * sb >= radius),
                                jnp.logical_or(li < n_blk - 1, jj + u * sb < lb + radius))
        outs, lses = [], []
        for h in range(B_HEADS):
            cs = slice(h * B_HEAD_DIM, (h + 1) * B_HEAD_DIM)
            s = lax.dot_general(kwin[rows, cs], q[u * sb:(u + 1) * sb, cs], (((1,), (1,)), ((), ())),
                                preferred_element_type=F32)
            s = jnp.where(valid, s + bias_ref[h], NEG)
            m = jnp.max(s, axis=0, keepdims=True)
            p = jnp.exp2(s - m)
            l = jnp.sum(p, axis=0, keepdims=True)
            o = lax.dot_general(vwin[rows, cs], p.astype(BF16), (((0,), (0,)), ((), ())),
                                preferred_element_type=F32)
            outs.append(o / l)
            lses.append(jnp.broadcast_to(m + jnp.log2(l), (B_HEAD_DIM, sb)))
        o_ref[u * sb:(u + 1) * sb, :] = jnp.concatenate(outs, axis=0).T
        lse_ref[u * sb:(u + 1) * sb, :] = jnp.concatenate(lses, axis=0).T


def _dilated_group(x, row0, b, s, prev, dilation, radius, cols):
    t = x.shape[0] * dilation
    n_col = 3
    assert x.shape[1] == dilation * n_col * B_GROUP_WIDTH
    sub_len = s // dilation
    lb = min(512, sub_len)
    sb = lb
    n_blk = sub_len // lb
    win = sb + 2 * radius
    assert row0 % (dilation * lb) == 0 and n_blk * lb == sub_len and lb % sb == 0
    blk0 = row0 // dilation // lb
    i = jnp.arange(sb, dtype=jnp.int32)[None, :]
    j = jnp.arange(win, dtype=jnp.int32)[:, None]
    rel_sub = j - radius - i
    bias = _bias_lookup(cols, rel_sub * dilation) * LOG2E
    bias = jnp.where((jnp.abs(rel_sub) <= radius)[None], bias, NEG)

    def spec(col, shift):
        def index(bi, r, li):
            return (blk0 + bi * n_blk + jnp.clip(li + shift, 0, n_blk - 1), r * n_col + col)
        return pl.BlockSpec((lb, B_GROUP_WIDTH), index)

    aliased = prev is not None
    kern = functools.partial(_dilated_kernel, lb=lb, sb=sb, radius=radius, n_blk=n_blk, aliased=aliased)
    out_spec = pl.BlockSpec((lb, B_GROUP_WIDTH), lambda bi, r, li: (blk0 + bi * n_blk + li, r))
    shape = jax.ShapeDtypeStruct((t // dilation, dilation * B_GROUP_WIDTH), F32)
    in_specs = [spec(0, 0),
                spec(1, -1), spec(1, 0), spec(1, 1),
                spec(2, -1), spec(2, 0), spec(2, 1),
                pl.BlockSpec((B_HEADS, win, sb), lambda bi, r, li: (0, 0, 0))]
    args = [x, x, x, x, x, x, x, bias]
    aliases = {}
    if aliased:
        in_specs += [pl.BlockSpec(memory_space=pl.ANY)] * 2
        args += list(prev)
        aliases = {len(args) - 2: 0, len(args) - 1: 1}
    return pl.pallas_call(
        kern,
        grid=(b, dilation, n_blk),
        in_specs=in_specs,
        out_specs=[out_spec, out_spec],
        out_shape=[shape, shape],
        input_output_aliases=aliases,
        compiler_params=_cparams(("parallel", "parallel", "arbitrary")),
    )(*args)


def _layer_norm(z, g, b):
    mu = jnp.mean(z, axis=-1, keepdims=True)
    zc = z - mu
    var = jnp.mean(zc * zc, axis=-1, keepdims=True)
    return zc * lax.rsqrt(var + LN_EPS) * g + b


def _route(logits):
    lane = lax.broadcasted_iota(jnp.int32, logits.shape, 1).astype(F32)
    far = float(LANES)
    is_group = lane < N_GROUPS
    gl = jnp.where(is_group, logits, NEG)
    gmax = jnp.max(gl, axis=1, keepdims=True)
    gidx = jnp.min(jnp.where(gl == gmax, lane, far), axis=1, keepdims=True)
    gden = jnp.sum(jnp.where(is_group, jnp.exp(gl - gmax), 0.0), axis=1, keepdims=True)
    g_w = 1.0 / gden
    lo = N_GROUPS + EXPERTS_PER_GROUP * gidx
    in_group = jnp.logical_and(lane >= lo, lane < lo + EXPERTS_PER_GROUP)
    el = jnp.where(in_group, logits, NEG)
    v1 = jnp.max(el, axis=1, keepdims=True)
    i1 = jnp.min(jnp.where(el == v1, lane, far), axis=1, keepdims=True)
    el2 = jnp.where(lane == i1, NEG, el)
    v2 = jnp.max(el2, axis=1, keepdims=True)
    i2 = jnp.min(jnp.where(el2 == v2, lane, far), axis=1, keepdims=True)
    e2 = jnp.exp(v2 - v1)
    den = 1.0 + e2
    gate0 = g_w / den
    gate1 = g_w * e2 / den
    out = jnp.where(lane == 0.0, gate0, 0.0)
    out = jnp.where(lane == 1.0, gate1, out)
    out = jnp.where(lane == 2.0, i1 - N_GROUPS, out)
    out = jnp.where(lane == 3.0, i2 - N_GROUPS, out)
    return out


def _post_mixer(x_ref, y, w_ref, g_ref, b_ref, wr_ref, x1_ref, route_ref):
    h = jnp.dot(y, w_ref[...], preferred_element_type=F32)
    x1 = _layer_norm(ALPHA * x_ref[...] + h, g_ref[...], b_ref[...])
    x1_ref[...] = x1
    route_ref[...] = _route(jnp.dot(x1.astype(BF16), wr_ref[...], preferred_element_type=F32))


def _post_diff_kernel(x_ref, o_ref, w_ref, g_ref, b_ref, wr_ref, x1_ref, route_ref):
    _post_mixer(x_ref, o_ref[...], w_ref, g_ref, b_ref, wr_ref, x1_ref, route_ref)


def _unfold(src_ref, scratch_ref, fold):
    if fold == 1:
        return src_ref[...]
    n_tiles, rows, _ = scratch_ref.shape
    for r in range(fold):
        for c in range(n_tiles):
            col = (r * n_tiles + c) * LANES
            scratch_ref[c, pl.ds(r, rows // fold, stride=fold), :] = src_ref[:, col:col + LANES]
    return jnp.concatenate([scratch_ref[c] for c in range(n_tiles)], axis=1)


def _post_dilated_kernel(x_ref, o0_ref, o1_ref, o2_ref, l0_ref, l1_ref, l2_ref, w_ref, g_ref, b_ref, wr_ref,
                         x1_ref, route_ref, *scratch):
    folds = [dil for (_, dil) in B_CONFIGS]
    o_refs, l_refs = (o0_ref, o1_ref, o2_ref), (l0_ref, l1_ref, l2_ref)
    o0, o1, o2 = [_unfold(o_refs[g], scratch[g], folds[g]) for g in range(B_GROUPS)]
    l0, l1, l2 = [_unfold(l_refs[g], scratch[B_GROUPS + g], folds[g]) for g in range(B_GROUPS)]
    m = jnp.maximum(jnp.maximum(l0, l1), l2)
    e0, e1, e2 = jnp.exp2(l0 - m), jnp.exp2(l1 - m), jnp.exp2(l2 - m)
    den = e0 + e1 + e2
    o = (o0 * (e0 / den) + o1 * (e1 / den)) + o2 * (e2 / den)
    _post_mixer(x_ref, o.astype(BF16), w_ref, g_ref, b_ref, wr_ref, x1_ref, route_ref)


def _post_mixer_call(kern, x, mixer_outs, w_out, gain, bias, w_route, scratch_widths=(), tm=512):
    t, d = x.shape
    row = lambda width: pl.BlockSpec((tm, width), lambda i: (i, 0))
    folded = lambda a: pl.BlockSpec((tm * a.shape[0] // t, a.shape[1]), lambda i: (i, 0))
    whole = lambda a: pl.BlockSpec(a.shape, lambda i: (0,) * a.ndim)
    return pl.pallas_call(
        kern,
        grid=(t // tm,),
        in_specs=[row(d)] + [folded(a) for a in mixer_outs]
        + [whole(w_out), whole(gain), whole(bias), whole(w_route)],
        out_specs=[row(d), row(LANES)],
        out_shape=[jax.ShapeDtypeStruct((t, d), F32), jax.ShapeDtypeStruct((t, LANES), F32)],
        scratch_shapes=[pltpu.VMEM((width // LANES, tm, LANES), F32) for width in scratch_widths],
        compiler_params=_cparams(("parallel",)),
    )(x, *mixer_outs, w_out, gain, bias, w_route)


def _moe_mlp_kernel(blk_ref, used_ref, src_ref, x_hbm, gw_ref, w1_ref, w3_ref, w2_ref, y_ref,
                    w1b_ref, w3b_ref, w2b_ref, xbuf0_ref, xbuf1_ref, xbuf2_ref, sem_ref):
    i = pl.program_id(0)
    n_used = used_ref[0]
    active = i < n_used
    bufs = (xbuf0_ref, xbuf1_ref, xbuf2_ref)
    ahead = MOE_ROW_BUFFERS - 1
    parity = lax.rem(i, MOE_ROW_BUFFERS)
    new_expert = jnp.logical_or(i == 0, blk_ref[i] != blk_ref[jnp.maximum(i - 1, 0)])

    def row_copy(block, r, slot):
        tok = src_ref[block * MOE_ROWS + r]
        return pltpu.make_async_copy(x_hbm.at[pl.ds(tok, 1), :], bufs[slot].at[pl.ds(r, 1), :], sem_ref.at[slot])

    def start_rows(block, slot):
        for r in range(MOE_ROWS):
            row_copy(block, r, slot).start()

    def wait_rows(block, slot):
        for r in range(MOE_ROWS):
            row_copy(block, r, slot).wait()

    @pl.when(i == 0)
    def _():
        for blk in range(ahead):
            start_rows(blk, blk)

    @pl.when(jnp.logical_and(active, new_expert))
    def _():
        w1b_ref[...] = w1_ref[...].astype(BF16)
        w3b_ref[...] = w3_ref[...].astype(BF16)
        w2b_ref[...] = w2_ref[...].astype(BF16)

    def compute_block(slot):
        wait_rows(i, slot)
        start_rows(i + ahead, (slot + ahead) % MOE_ROW_BUFFERS)
        x = bufs[slot][...].astype(BF16)
        h1 = jnp.dot(x, w1b_ref[...], preferred_element_type=F32)
        h3 = jnp.dot(x, w3b_ref[...], preferred_element_type=F32)
        h = (h1 * jax.nn.sigmoid(h1)) * h3
        y = jnp.dot(h.astype(BF16), w2b_ref[...], preferred_element_type=F32)
        y_ref[...] = (y * gw_ref[...]).astype(y_ref.dtype)

    draining = jnp.logical_and(i >= n_used, i < n_used + ahead)
    for slot in range(MOE_ROW_BUFFERS):
        pl.when(jnp.logical_and(active, parity == slot))(functools.partial(compute_block, slot))
        pl.when(jnp.logical_and(draining, parity == slot))(functools.partial(wait_rows, i, slot))

    @pl.when(jnp.logical_not(active))
    def _():
        y_ref[...] = jnp.zeros(y_ref.shape, y_ref.dtype)


def _moe_mlp(x, src_tok, gw, blk_exp, n_used, layer, w1, w3, w2):
    p = src_tok.shape[0]
    d = x.shape[1]
    n_blk = p // MOE_ROWS
    grid_spec = pltpu.PrefetchScalarGridSpec(
        num_scalar_prefetch=3,
        grid=(n_blk,),
        in_specs=[pl.BlockSpec(memory_space=pl.ANY),
                  pl.BlockSpec((MOE_ROWS, 1), lambda i, be, nu, st: (i, 0)),
                  pl.BlockSpec((None, None, d, D_EXPERT), lambda i, be, nu, st: (layer, be[i], 0, 0)),
                  pl.BlockSpec((None, None, d, D_EXPERT), lambda i, be, nu, st: (layer, be[i], 0, 0)),
                  pl.BlockSpec((None, None, D_EXPERT, d), lambda i, be, nu, st: (layer, be[i], 0, 0))],
        out_specs=pl.BlockSpec((MOE_ROWS, d), lambda i, be, nu, st: (i, 0)),
        scratch_shapes=[pltpu.VMEM((d, D_EXPERT), BF16), pltpu.VMEM((d, D_EXPERT), BF16),
                        pltpu.VMEM((D_EXPERT, d), BF16),
                        pltpu.VMEM((MOE_ROWS, d), F32), pltpu.VMEM((MOE_ROWS, d), F32),
                        pltpu.VMEM((MOE_ROWS, d), F32),
                        pltpu.SemaphoreType.DMA((MOE_ROW_BUFFERS,))],
    )
    return pl.pallas_call(
        _moe_mlp_kernel,
        grid_spec=grid_spec,
        out_shape=jax.ShapeDtypeStruct((p, d), BF16),
        compiler_params=_cparams(("arbitrary",)),
    )(blk_exp, n_used, src_tok, x, gw, w1, w3, w2)


def _combine_ln_kernel(x_ref, ya_ref, yb_ref, g_ref, b_ref, *rest):
    o_ref = rest[-1]
    z = ALPHA * x_ref[...] + (ya_ref[...].astype(F32) + yb_ref[...].astype(F32))
    o_ref[...] = _layer_norm(z, g_ref[...], b_ref[...])


def _combine_ln(x1, row0, ya, yb, gain, bias, prev_out, tm=1024):
    t, d = x1.shape
    n = ya.shape[0]
    assert row0 % tm == 0 and n % tm == 0
    blk0 = row0 // tm
    row_x = pl.BlockSpec((tm, d), lambda i: (blk0 + i, 0))
    row_y = pl.BlockSpec((tm, d), lambda i: (i, 0))
    vec = pl.BlockSpec((1, d), lambda i: (0, 0))
    in_specs = [row_x, row_y, row_y, vec, vec]
    args = [x1, ya, yb, gain, bias]
    aliases = {}
    if prev_out is not None:
        in_specs.append(pl.BlockSpec(memory_space=pl.ANY))
        args.append(prev_out)
        aliases = {len(args) - 1: 0}
    return pl.pallas_call(
        _combine_ln_kernel,
        grid=(n // tm,),
        in_specs=in_specs,
        out_specs=row_x,
        out_shape=jax.ShapeDtypeStruct((t, d), F32),
        input_output_aliases=aliases,
        compiler_params=_cparams(("parallel",)),
    )(*args)


def _moe(x1, route, row0, n_tok, prev_out, layer, w1, w3, w2, gain, bias):
    t = n_tok
    n_assign = 2 * t
    take = lambda a, idx: a.at[idx].get(mode='promise_in_bounds')
    route = lax.slice_in_dim(route, row0, row0 + t, axis=0)
    gate = route[:, 0:2].reshape(n_assign)
    e = route[:, 2:4].astype(jnp.int32).reshape(n_assign)
    order = jnp.argsort(e).astype(jnp.int32)
    rank = jnp.argsort(order).astype(jnp.int32)
    experts = jnp.arange(N_EXPERTS, dtype=jnp.int32)
    counts = jnp.sum((e[:, None] == experts[None, :]).astype(jnp.int32), axis=0)
    padded = (counts + MOE_ROWS - 1) // MOE_ROWS * MOE_ROWS
    starts = jnp.cumsum(counts) - counts
    pends = jnp.cumsum(padded)
    pstarts = pends - padded
    pos = (take(pstarts - starts, e) + rank).reshape(t, 2)
    n_rows = n_assign + (N_EXPERTS + MOE_ROW_BUFFERS - 2) * MOE_ROWS
    n_blk = n_rows // MOE_ROWS
    blk_row0 = jnp.arange(n_blk, dtype=jnp.int32) * MOE_ROWS
    blk_exp = jnp.minimum(jnp.sum((pends[None, :] <= blk_row0[:, None]).astype(jnp.int32), axis=1), N_EXPERTS - 1)
    n_used = (pends[N_EXPERTS - 1] // MOE_ROWS).astype(jnp.int32)[None]
    base = jnp.repeat(take(starts - pstarts, blk_exp), MOE_ROWS)
    limit = jnp.repeat(take(starts + counts, blk_exp), MOE_ROWS)
    sidx = jnp.arange(n_rows, dtype=jnp.int32) + base
    valid = sidx < limit
    a_src = take(order, jnp.clip(sidx, 0, n_assign - 1))
    src_tok = jnp.where(valid, a_src // 2, 0) + row0
    gw = jnp.where(valid, take(gate, a_src), 0.0)
    y = _moe_mlp(x1, src_tok, gw[:, None], blk_exp, n_used, layer, w1, w3, w2)
    ya = take(y, pos[:, 0])
    yb = take(y, pos[:, 1])
    return _combine_ln(x1, row0, ya, yb, gain, bias, prev_out)


def _col_scale(width, n_scaled, scale):
    return jnp.where(jnp.arange(width) < n_scaled, scale, 1.0).astype(F32)[None, :]


def _trunk(x, batch_shapes, rel_bias_table, a_w_in, a_w_out, a_lambda_q1, a_lambda_k1, a_lambda_q2, a_lambda_k2,
           a_subln_g, b_w_in, b_w_out, ln_gain, ln_bias, router_group, router_expert,
           expert_w1, expert_w3, expert_w2):
    a_cols = rel_bias_table[:, :A_HEADS].T
    bias_tiles = _diff_bias_tiles(a_cols)
    a_scale = _col_scale(3 * A_QK_WIDTH, A_QK_WIDTH, A_HEAD_DIM ** -0.5 * LOG2E)
    b_scale = _col_scale(3 * B_GROUP_WIDTH, B_GROUP_WIDTH, B_HEAD_DIM ** -0.5 * LOG2E)
    offsets = []
    start = 0
    for (b, s) in batch_shapes:
        offsets.append((start, b, s))
        start += b * s
    t = x.shape[0]

    for i in range(DEPTH):
        li = i // N_MIXERS
        w_route = jnp.concatenate(
            [router_group[i], router_expert[i],
             jnp.zeros((D_MODEL, LANES - N_GROUPS - N_EXPERTS), F32)], axis=1).astype(BF16)
        gain1, bias1 = ln_gain[i, 0][None, :], ln_bias[i, 0][None, :]
        gain2, bias2 = ln_gain[i, 1][None, :], ln_bias[i, 1][None, :]
        if i % N_MIXERS == 0:
            lambda_init = 0.8 - 0.6 * math.exp(-0.3 * i)
            qkv = _project(x, a_w_in[li].astype(BF16), a_scale)
            lam_vecs = jnp.stack([a_lambda_q1[li], a_lambda_k1[li], a_lambda_q2[li], a_lambda_k2[li]]).astype(F32)
            subg_col = a_subln_g[li].astype(F32)[:, None]
            o = None
            for (st, b, s) in offsets:
                o = _diff_attention(qkv, st, b, s, o, bias_tiles, lam_vecs, subg_col, lambda_init)
            mixer_outs = [o]
            kern = _post_diff_kernel
            w_out = a_w_out[li].astype(BF16)
            scratch_widths = ()
        else:
            outs, lses = [], []
            for g, (window, dil) in enumerate(B_CONFIGS):
                c0 = A_HEADS + g * B_HEADS
                cols = rel_bias_table[:, c0:c0 + B_HEADS].T
                w_g = jnp.concatenate([b_w_in[li][:, (sec * B_GROUPS + g) * B_GROUP_WIDTH:
                                                  (sec * B_GROUPS + g + 1) * B_GROUP_WIDTH] for sec in range(3)],
                                      axis=1).astype(BF16)
                qkv_g = _project_folded(x, w_g, b_scale, dil)
                pair = None
                for (st, b, s) in offsets:
                    pair = _dilated_group(qkv_g, st, b, s, pair, dil, window // (2 * dil), cols)
                outs.append(pair[0])
                lses.append(pair[1])
            mixer_outs = outs + lses
            kern = _post_dilated_kernel
            w_out = b_w_out[li].astype(BF16)
            scratch_widths = (B_GROUP_WIDTH,) * (2 * B_GROUPS)
        x1, route = _post_mixer_call(kern, x, mixer_outs, w_out, gain1, bias1, w_route, scratch_widths)
        x = _moe(x1, route, 0, t, None, i, expert_w1, expert_w3, expert_w2, gain2, bias2)
    return x


def kernel(x_prompt, x_sample, rel_bias_table, a_w_in, a_w_out, a_lambda_q1, a_lambda_k1, a_lambda_q2, a_lambda_k2, a_subln_g, b_w_in, b_w_out, ln_gain, ln_bias, router_group, router_expert, expert_w1, expert_w3, expert_w2):
    shapes = [x_prompt.shape[:2], x_sample.shape[:2]]
    x = jnp.concatenate([x_prompt.reshape(-1, D_MODEL), x_sample.reshape(-1, D_MODEL)], axis=0)
    y = _trunk(x, shapes, rel_bias_table, a_w_in, a_w_out, a_lambda_q1, a_lambda_k1, a_lambda_q2, a_lambda_k2,
               a_subln_g, b_w_in, b_w_out, ln_gain, ln_bias, router_group, router_expert,
               expert_w1, expert_w3, expert_w2)
    n_prompt = x_prompt.shape[0] * x_prompt.shape[1]
    return (y[:n_prompt].reshape(x_prompt.shape), y[n_prompt:].reshape(x_sample.shape))
```
